```python
import math
import jax, jax.numpy as jnp
from jax import lax
import numpy as np

D_MODEL = 2048
BATCH = 1
SEQ = 16384
DEPTH = 1
DEC_BATCH = 1
DEC_SEQ = 8192
PAST_LEN = 128

POOL_WIDTH = D_MODEL // 2
POOL_GROUPS = 4
POOL_GROUP_WIDTH = POOL_WIDTH // POOL_GROUPS
POOL_WINDOWS = (2, 4, 8, 16)
ATTN_HEADS = 8
ATTN_HEAD_DIM = 64
ATTN_V_DIM = 2 * ATTN_HEAD_DIM
QK_WIDTH = ATTN_HEADS * 2 * ATTN_HEAD_DIM
ATTN_WIDTH = ATTN_HEADS * ATTN_V_DIM
MIX_WIDTH = POOL_WIDTH + ATTN_WIDTH
IN_PROJ_WIDTH = POOL_WIDTH + 2 * QK_WIDTH + ATTN_WIDTH
ROPE_THETA = 10000.0
Q_BLOCK = 128
PEER_HEADS = 8
PEER_KEYS = 128
PEER_EXPERTS = PEER_KEYS * PEER_KEYS
PEER_QUERY_DIM = 256
PEER_TOPK = 16
TOKEN_BLOCK = 128
EPS = 1e-6

kernel_name = "hybrid_pool_diffattn_peer_encoder"


def _rmsnorm(x, g):
    xf = x.astype(jnp.float32)
    y = xf * lax.rsqrt(jnp.mean(xf * xf, axis=-1, keepdims=True) + EPS)
    return (y * g.astype(jnp.float32)).astype(x.dtype)


def _rope(x, seq_len):
    half = ATTN_HEAD_DIM // 2
    inv = 1.0 / (ROPE_THETA ** (jnp.arange(half, dtype=jnp.float32) / half))
    ang = jnp.arange(seq_len, dtype=jnp.float32)[:, None] * inv[None, :]
    cos = jnp.cos(ang)[None, :, None, None, :]
    sin = jnp.sin(ang)[None, :, None, None, :]
    xf = x.astype(jnp.float32)
    x1, x2 = xf[..., :half], xf[..., half:]
    out = jnp.concatenate([x1 * cos - x2 * sin, x2 * cos + x1 * sin], axis=-1)
    return out.astype(x.dtype)


def _pool_mixer(xp, pool_w, pool_b, pool_scale):
    B, S, _ = xp.shape
    xg = xp.reshape(B, S, POOL_GROUPS, POOL_GROUP_WIDTH).astype(jnp.float32)
    csum = jnp.concatenate(
        [jnp.zeros((B, 1, POOL_GROUPS, POOL_GROUP_WIDTH), jnp.float32), jnp.cumsum(xg, axis=1)], axis=1)
    t = jnp.arange(S, dtype=jnp.int32)
    means = []
    for gi, w in enumerate(POOL_WINDOWS):
        lo = jnp.maximum(t - w // 2, 0)
        hi = jnp.minimum(t + w // 2, S)
        cg = csum[:, :, gi]
        s = jnp.take(cg, hi, axis=1) - jnp.take(cg, lo, axis=1)
        means.append(s / (hi - lo).astype(jnp.float32)[None, :, None])
    pooled = (jnp.stack(means, axis=2) - xg).astype(xp.dtype)
    y = jnp.einsum('bsgc,gce->bsge', pooled, pool_w) + pool_b
    return y.reshape(B, S, POOL_WIDTH) * pool_scale


def _diff_attention(q, k, v, lam, lambda_init, subln_g):
    B, S = q.shape[0], q.shape[1]
    nb = S // Q_BLOCK
    kt = k.transpose(0, 2, 3, 1, 4)
    vt = v.transpose(0, 2, 1, 3)
    qb = q.reshape(B, nb, Q_BLOCK, ATTN_HEADS, 2, ATTN_HEAD_DIM).transpose(1, 0, 3, 4, 2, 5)
    scale = ATTN_HEAD_DIM ** -0.5

    def block(qblk):
        s = jnp.einsum('bhcqd,bhckd->bhcqk', qblk, kt,
                       preferred_element_type=jnp.float32) * scale
        p = jax.nn.softmax(s, axis=-1)
        a = p[:, :, 0] - lam * p[:, :, 1]
        return jnp.einsum('bhqk,bhkd->bhqd', a.astype(vt.dtype), vt)

    o = lax.map(block, qb)
    o = o.transpose(1, 0, 3, 2, 4).reshape(B, S, ATTN_HEADS, ATTN_V_DIM)
    o = _rmsnorm(o, subln_g) * (1.0 - lambda_init)
    return o.reshape(B, S, ATTN_WIDTH).astype(v.dtype)


def _peer(x, peer_wq, peer_keys, peer_u, peer_v):
    B, S, D = x.shape
    T = B * S
    xt = x.reshape(T, D)
    q = (xt @ peer_wq).reshape(T, PEER_HEADS, 2, PEER_QUERY_DIM // 2)
    sc = jnp.einsum('thcd,hcnd->thcn', q, peer_keys)
    s1, i1 = lax.top_k(sc[:, :, 0], PEER_TOPK)
    s2, i2 = lax.top_k(sc[:, :, 1], PEER_TOPK)
    cand = (s1[..., :, None] + s2[..., None, :]).reshape(T, PEER_HEADS, PEER_TOPK * PEER_TOPK)
    cand_idx = (i1[..., :, None] * PEER_KEYS + i2[..., None, :]).reshape(T, PEER_HEADS, PEER_TOPK * PEER_TOPK)
    top_s, pos = lax.top_k(cand, PEER_TOPK)
    idx = jnp.take_along_axis(cand_idx, pos, axis=-1)
    gate = jax.nn.softmax(top_s.astype(jnp.float32), axis=-1).astype(x.dtype)
    nb = T // TOKEN_BLOCK

    def block(args):
        xb, ib, gb = args
        u = peer_u[ib]
        v = peer_v[ib]
        h = jnp.einsum('thkd,td->thk', u, xb)
        return jnp.einsum('thk,thkd->td', gb * jax.nn.gelu(h, approximate=False), v)

    out = lax.map(block, (xt.reshape(nb, TOKEN_BLOCK, D),
                          idx.reshape(nb, TOKEN_BLOCK, PEER_HEADS, PEER_TOPK),
                          gate.reshape(nb, TOKEN_BLOCK, PEER_HEADS, PEER_TOPK)))
    return out.reshape(B, S, D)


def _layer(x, layer_idx, norm1_g, w_in, pool_w, pool_b, pool_scale,
           lambda_q1, lambda_k1, lambda_q2, lambda_k2, subln_g, w_o,
           norm2_g, peer_wq, peer_keys, peer_u, peer_v):
    B, S, _ = x.shape
    h = _rmsnorm(x, norm1_g)
    z = h @ w_in
    xp = z[..., :POOL_WIDTH]
    q = z[..., POOL_WIDTH:POOL_WIDTH + QK_WIDTH].reshape(B, S, ATTN_HEADS, 2, ATTN_HEAD_DIM)
    k = z[..., POOL_WIDTH + QK_WIDTH:POOL_WIDTH + 2 * QK_WIDTH].reshape(B, S, ATTN_HEADS, 2, ATTN_HEAD_DIM)
    v = z[..., POOL_WIDTH + 2 * QK_WIDTH:].reshape(B, S, ATTN_HEADS, ATTN_V_DIM)
    pool_out = _pool_mixer(xp, pool_w, pool_b, pool_scale)
    lambda_init = 0.8 - 0.6 * math.exp(-0.3 * layer_idx)
    lam = (jnp.exp(jnp.sum(lambda_q1.astype(jnp.float32) * lambda_k1.astype(jnp.float32)))
           - jnp.exp(jnp.sum(lambda_q2.astype(jnp.float32) * lambda_k2.astype(jnp.float32)))
           + lambda_init)
    attn_out = _diff_attention(_rope(q, S), _rope(k, S), v, lam, lambda_init, subln_g)
    mixed = jnp.concatenate([pool_out.astype(x.dtype), attn_out], axis=-1)
    x = x + mixed @ w_o
    x = x + _peer(_rmsnorm(x, norm2_g), peer_wq, peer_keys, peer_u, peer_v)
    return x


def _trunk(x, norm1_g, w_in, pool_w, pool_b, pool_scale, lambda_q1, lambda_k1,
           lambda_q2, lambda_k2, subln_g, w_o, norm2_g, peer_wq, peer_keys,
           peer_u, peer_v, final_norm_g):
    for l in range(DEPTH):
        x = _layer(x, l, norm1_g[l], w_in[l], pool_w[l], pool_b[l], pool_scale[l],
                   lambda_q1[l], lambda_k1[l], lambda_q2[l], lambda_k2[l], subln_g[l],
                   w_o[l], norm2_g[l], peer_wq[l], peer_keys[l], peer_u[l], peer_v[l])
    return _rmsnorm(x, final_norm_g)


def setup_inputs(seed: int = 0) -> dict:
    key = jax.random.key(seed)
    ks = jax.random.split(key, 20)
    f32 = jnp.float32
    L = DEPTH

    def nrm(k, shape, scale):
        return jax.random.normal(k, shape, f32) * scale

    return {
        "x_prompt": nrm(ks[0], (BATCH, SEQ, D_MODEL), 1.0),
        "x_sample": nrm(ks[1], (DEC_BATCH, DEC_SEQ, D_MODEL), 1.0),
        "norm1_g": 1.0 + nrm(ks[2], (L, D_MODEL), 0.05),
        "w_in": nrm(ks[3], (L, D_MODEL, IN_PROJ_WIDTH), D_MODEL ** -0.5),
        "pool_w": nrm(ks[4], (L, POOL_GROUPS, POOL_GROUP_WIDTH, POOL_GROUP_WIDTH), POOL_GROUP_WIDTH ** -0.5),
        "pool_b": nrm(ks[5], (L, POOL_GROUPS, POOL_GROUP_WIDTH), 0.01),
        "pool_scale": 1.0 + nrm(ks[6], (L, POOL_WIDTH), 0.1),
        "lambda_q1": nrm(ks[7], (L, ATTN_HEAD_DIM), 0.1),
        "lambda_k1": nrm(ks[8], (L, ATTN_HEAD_DIM), 0.1),
        "lambda_q2": nrm(ks[9], (L, ATTN_HEAD_DIM), 0.1),
        "lambda_k2": nrm(ks[10], (L, ATTN_HEAD_DIM), 0.1),
        "subln_g": 1.0 + nrm(ks[11], (L, ATTN_V_DIM), 0.05),
        "w_o": nrm(ks[12], (L, MIX_WIDTH, D_MODEL), MIX_WIDTH ** -0.5),
        "norm2_g": 1.0 + nrm(ks[13], (L, D_MODEL), 0.05),
        "peer_wq": nrm(ks[14], (L, D_MODEL, PEER_HEADS * PEER_QUERY_DIM), D_MODEL ** -0.5),
        "peer_keys": nrm(ks[15], (L, PEER_HEADS, 2, PEER_KEYS, PEER_QUERY_DIM // 2), (PEER_QUERY_DIM // 2) ** -0.5),
        "peer_u": nrm(ks[16], (L, PEER_EXPERTS, D_MODEL), D_MODEL ** -0.5),
        "peer_v": nrm(ks[17], (L, PEER_EXPERTS, D_MODEL), PEER_HEADS ** -0.5),
        "final_norm_g": 1.0 + nrm(ks[18], (D_MODEL,), 0.05),
    }


def reference(x_prompt, x_sample, norm1_g, w_in, pool_w, pool_b, pool_scale,
              lambda_q1, lambda_k1, lambda_q2, lambda_k2, subln_g, w_o, norm2_g,
              peer_wq, peer_keys, peer_u, peer_v, final_norm_g):
    y_prompt = _trunk(x_prompt, norm1_g, w_in, pool_w, pool_b, pool_scale, lambda_q1,
                      lambda_k1, lambda_q2, lambda_k2, subln_g, w_o, norm2_g, peer_wq,
                      peer_keys, peer_u, peer_v, final_norm_g)
    y_sample = _trunk(x_sample, norm1_g, w_in, pool_w, pool_b, pool_scale, lambda_q1,
                      lambda_k1, lambda_q2, lambda_k2, subln_g, w_o, norm2_g, peer_wq,
                      peer_keys, peer_u, peer_v, final_norm_g)
    return (y_prompt, y_sample)
```

```python
import functools
import math

import jax
import jax.numpy as jnp
from jax import lax
from jax.experimental import pallas as pl
from jax.experimental.pallas import tpu as pltpu

F32 = jnp.float32
BF16 = jnp.bfloat16

D_MODEL = 2048
POOL_WIDTH = 1024
POOL_GROUPS = 4
POOL_GROUP_WIDTH = 256
POOL_WINDOWS = (2, 4, 8, 16)
POOL_HALO = 8
ATTN_HEADS = 8
ATTN_HEAD_DIM = 64
ATTN_V_DIM = 128
QK_WIDTH = 1024
ATTN_WIDTH = 1024
IN_PROJ_WIDTH = 4096
ROPE_THETA = 10000.0
PEER_HEADS = 8
PEER_KEYS = 128
PEER_EXPERTS = PEER_KEYS * PEER_KEYS
PEER_TOPK = 16
PEER_SLOTS = PEER_HEADS * PEER_TOPK
EPS = 1e-6
LAMBDA_INIT = 0.8 - 0.6 * math.exp(-0.3 * 0)

LANES = 128
VMEM_LIMIT = 56 * 1024 * 1024


def _cparams(sem):
    return pltpu.CompilerParams(dimension_semantics=sem, vmem_limit_bytes=VMEM_LIMIT)


IN_TM = 512
IN_TN = 512


def _inproj_kernel(x_ref, g_ref, w_ref, cos_ref, sin_ref, z_ref, h_scr):
    j = pl.program_id(1)

    @pl.when(j == 0)
    def _():
        x = x_ref[...]
        ms = jnp.mean(x * x, axis=-1, keepdims=True)
        h_scr[...] = (x * lax.rsqrt(ms + EPS) * g_ref[...]).astype(BF16)

    acc = jnp.dot(h_scr[...], w_ref[...], preferred_element_type=F32)
    q_lo, k_lo, v_lo = (POOL_WIDTH // IN_TN, (POOL_WIDTH + QK_WIDTH) // IN_TN,
                        (POOL_WIDTH + 2 * QK_WIDTH) // IN_TN)
    is_rope = jnp.logical_and(j >= q_lo, j < v_lo)

    @pl.when(is_rope)
    def _():
        cos = cos_ref[...]
        sin = sin_ref[...]
        scale = jnp.where(j < k_lo, ATTN_HEAD_DIM ** -0.5, 1.0).astype(F32)
        lane = lax.broadcasted_iota(jnp.int32, cos.shape, 1)
        lower = (lane % ATTN_HEAD_DIM) < (ATTN_HEAD_DIM // 2)
        for c in range(IN_TN // LANES):
            xc = acc[:, c * LANES:(c + 1) * LANES]
            up = pltpu.roll(xc, LANES - ATTN_HEAD_DIM // 2, 1)
            dn = pltpu.roll(xc, ATTN_HEAD_DIM // 2, 1)
            rot = jnp.where(lower, -up, dn)
            z_ref[:, c * LANES:(c + 1) * LANES] = ((xc * cos + rot * sin) * scale).astype(BF16)

    @pl.when(jnp.logical_not(is_rope))
    def _():
        z_ref[...] = acc.astype(BF16)


def _in_proj(x, g, w_bf, cos, sin):
    S = x.shape[0]
    return pl.pallas_call(
        _inproj_kernel,
        grid=(S // IN_TM, IN_PROJ_WIDTH // IN_TN),
        in_specs=[
            pl.BlockSpec((IN_TM, D_MODEL), lambda i, j: (i, 0)),
            pl.BlockSpec((1, D_MODEL), lambda i, j: (0, 0)),
            pl.BlockSpec((D_MODEL, IN_TN), lambda i, j: (0, j)),
            pl.BlockSpec((IN_TM, LANES), lambda i, j: (i, 0)),
            pl.BlockSpec((IN_TM, LANES), lambda i, j: (i, 0)),
        ],
        out_specs=pl.BlockSpec((IN_TM, IN_TN), lambda i, j: (i, j)),
        out_shape=jax.ShapeDtypeStruct((S, IN_PROJ_WIDTH), BF16),
        scratch_shapes=[pltpu.VMEM((IN_TM, D_MODEL), BF16)],
        compiler_params=_cparams(("arbitrary", "arbitrary")),
        name="in_proj",
    )(x, g, w_bf, cos, sin)


AT_TQ = 256
AT_TK = 512


def _attn_kernel(q_ref, k_ref, v_ref, lam_ref, g_ref, o_ref, m_scr, l_scr, acc_scr):
    S = k_ref.shape[0]
    q = q_ref[...]
    lane = lax.broadcasted_iota(jnp.int32, q.shape, 1)
    zero = jnp.zeros_like(q)
    qs = jnp.concatenate([jnp.where(lane < ATTN_HEAD_DIM, q, zero),
                          jnp.where(lane >= ATTN_HEAD_DIM, q, zero)], axis=0)
    m_scr[...] = jnp.full(m_scr.shape, -jnp.inf, F32)
    l_scr[...] = jnp.zeros(l_scr.shape, F32)
    acc_scr[...] = jnp.zeros(acc_scr.shape, F32)

    def body(c, carry):
        off = pl.multiple_of(c * AT_TK, AT_TK)
        kc = k_ref[pl.ds(off, AT_TK), :]
        vc = v_ref[pl.ds(off, AT_TK), :]
        s = lax.dot_general(qs, kc, (((1,), (1,)), ((), ())), preferred_element_type=F32)
        m_old = m_scr[...]
        m_new = jnp.maximum(m_old, jnp.max(s, axis=-1, keepdims=True))
        alpha = jnp.exp(m_old - m_new)
        p = jnp.exp(s - m_new)
        l_scr[...] = alpha * l_scr[...] + jnp.sum(p, axis=-1, keepdims=True)
        acc_scr[...] = alpha * acc_scr[...] + jnp.dot(p.astype(BF16), vc, preferred_element_type=F32)
        m_scr[...] = m_new
        return carry

    lax.fori_loop(0, S // AT_TK, body, 0)

    lv = lam_ref[...]
    lam = (jnp.exp(jnp.sum(lv[0:1] * lv[1:2], axis=-1, keepdims=True))
           - jnp.exp(jnp.sum(lv[2:3] * lv[3:4], axis=-1, keepdims=True)) + LAMBDA_INIT)
    o = acc_scr[...] / l_scr[...]
    d = o[:AT_TQ] - lam * o[AT_TQ:]
    ms = jnp.mean(d * d, axis=-1, keepdims=True)
    y = d * lax.rsqrt(ms + EPS) * g_ref[...] * (1.0 - LAMBDA_INIT)
    o_ref[...] = y.astype(BF16)


def _attention(z, lam_vecs, subln_g):
    S = z.shape[0]
    qb, kb, vb = POOL_WIDTH // LANES, (POOL_WIDTH + QK_WIDTH) // LANES, (POOL_WIDTH + 2 * QK_WIDTH) // LANES
    return pl.pallas_call(
        _attn_kernel,
        grid=(ATTN_HEADS, S // AT_TQ),
        in_specs=[
            pl.BlockSpec((AT_TQ, LANES), lambda h, i: (i, qb + h)),
            pl.BlockSpec((S, LANES), lambda h, i: (0, kb + h)),
            pl.BlockSpec((S, LANES), lambda h, i: (0, vb + h)),
            pl.BlockSpec((4, ATTN_HEAD_DIM), lambda h, i: (0, 0)),
            pl.BlockSpec((1, ATTN_V_DIM), lambda h, i: (0, 0)),
        ],
        out_specs=pl.BlockSpec((AT_TQ, LANES), lambda h, i: (i, h)),
        out_shape=jax.ShapeDtypeStruct((S, ATTN_WIDTH), BF16),
        scratch_shapes=[pltpu.VMEM((2 * AT_TQ, 1), F32), pltpu.VMEM((2 * AT_TQ, 1), F32),
                        pltpu.VMEM((2 * AT_TQ, ATTN_V_DIM), F32)],
        compiler_params=_cparams(("arbitrary", "arbitrary")),
        name="attn",
    )(z, z, z, lam_vecs, subln_g)


MX_TM = 256
MX_HB = 16


def _mix_kernel(x_ref, zp_ref, zm_ref, zn_ref, a_ref, pw_ref, pb_ref, ps_ref, wo_ref, g2_ref,
                x1_ref, xn_ref, *, seq_len):
    i = pl.program_id(0)
    r0 = i * MX_TM
    main = zm_ref[...].astype(F32)
    ext = jnp.concatenate([zp_ref[...].astype(F32)[MX_HB - POOL_HALO:], main,
                           zn_ref[...].astype(F32)[:POOL_HALO]], axis=0)
    ext_bf = ext.astype(BF16)
    ne = MX_TM + 2 * POOL_HALO
    t = r0 + lax.broadcasted_iota(jnp.int32, (MX_TM, ne), 0)
    r = r0 - POOL_HALO + lax.broadcasted_iota(jnp.int32, (MX_TM, ne), 1)
    tcol = r0 + lax.broadcasted_iota(jnp.int32, (MX_TM, 1), 0)
    acc = jnp.dot(a_ref[...], wo_ref[POOL_WIDTH:, :], preferred_element_type=F32)
    for gi, w in enumerate(POOL_WINDOWS):
        lo = jnp.maximum(t - w // 2, 0)
        hi = jnp.minimum(t + w // 2, seq_len)
        band = jnp.where(jnp.logical_and(r >= lo, r < hi), 1.0, 0.0).astype(BF16)
        cnt = (jnp.minimum(tcol + w // 2, seq_len) - jnp.maximum(tcol - w // 2, 0)).astype(F32)
        cs = slice(gi * POOL_GROUP_WIDTH, (gi + 1) * POOL_GROUP_WIDTH)
        wsum = jnp.dot(band, ext_bf[:, cs], preferred_element_type=F32)
        pooled = wsum / cnt - main[:, cs]
        y = jnp.dot(pooled.astype(BF16), pw_ref[gi], preferred_element_type=F32)
        y = (y + pb_ref[:, cs]) * ps_ref[:, cs]
        acc = acc + jnp.dot(y.astype(BF16), wo_ref[cs, :], preferred_element_type=F32)
    x1 = x_ref[...] + acc
    x1_ref[...] = x1
    ms = jnp.mean(x1 * x1, axis=-1, keepdims=True)
    xn_ref[...] = x1 * lax.rsqrt(ms + EPS) * g2_ref[...]


def _mix(x, z, a, pool_w_bf, pool_b, pool_scale, w_o_bf, norm2_g):
    S = x.shape[0]
    hb = MX_TM // MX_HB
    nhb = S // MX_HB
    return pl.pallas_call(
        functools.partial(_mix_kernel, seq_len=S),
        grid=(S // MX_TM,),
        in_specs=[
            pl.BlockSpec((MX_TM, D_MODEL), lambda i: (i, 0)),
            pl.BlockSpec((MX_HB, POOL_WIDTH), lambda i: (jnp.maximum(i * hb - 1, 0), 0)),
            pl.BlockSpec((MX_TM, POOL_WIDTH), lambda i: (i, 0)),
            pl.BlockSpec((MX_HB, POOL_WIDTH), lambda i: (jnp.minimum((i + 1) * hb, nhb - 1), 0)),
            pl.BlockSpec((MX_TM, ATTN_WIDTH), lambda i: (i, 0)),
            pl.BlockSpec((POOL_GROUPS, POOL_GROUP_WIDTH, POOL_GROUP_WIDTH), lambda i: (0, 0, 0)),
            pl.BlockSpec((1, POOL_WIDTH), lambda i: (0, 0)),
            pl.BlockSpec((1, POOL_WIDTH), lambda i: (0, 0)),
            pl.BlockSpec((D_MODEL, D_MODEL), lambda i: (0, 0)),
            pl.BlockSpec((1, D_MODEL), lambda i: (0, 0)),
        ],
        out_specs=[pl.BlockSpec((MX_TM, D_MODEL), lambda i: (i, 0)),
                   pl.BlockSpec((MX_TM, D_MODEL), lambda i: (i, 0))],
        out_shape=[jax.ShapeDtypeStruct((S, D_MODEL), F32), jax.ShapeDtypeStruct((S, D_MODEL), F32)],
        compiler_params=_cparams(("arbitrary",)),
        name="mix",
    )(x, z, z, z, a, pool_w_bf, pool_b, pool_scale, w_o_bf, norm2_g)


RT_TM = 256


def _top1(s, pos, sentinel):
    m = jnp.max(s, axis=0, keepdims=True)
    p = jnp.min(jnp.where(s == m, pos, sentinel), axis=0, keepdims=True)
    return m, p


def _route_kernel(xn_ref, wq_ref, keys_ref, idx_ref, gate_ref):
    q = jnp.dot(xn_ref[...].astype(BF16), wq_ref[...], preferred_element_type=F32)
    krow = lax.broadcasted_iota(jnp.int32, (PEER_KEYS, RT_TM), 0)
    nc = PEER_TOPK * PEER_TOPK
    crow = lax.broadcasted_iota(jnp.int32, (nc, RT_TM), 0)
    for h in range(PEER_HEADS):
        sv, si = [], []
        for c in range(2):
            hc = 2 * h + c
            qc = q[:, hc * PEER_KEYS:(hc + 1) * PEER_KEYS].astype(BF16)
            s = lax.dot_general(keys_ref[hc], qc, (((1,), (1,)), ((), ())),
                                preferred_element_type=F32)
            vals, ids = [], []
            for _ in range(PEER_TOPK):
                m, p = _top1(s, krow, PEER_KEYS)
                vals.append(m)
                ids.append(p)
                s = jnp.where(krow == p, -jnp.inf, s)
            sv.append(vals)
            si.append(ids)
        s2 = jnp.concatenate(sv[1], axis=0)
        i2 = jnp.concatenate(si[1], axis=0)
        cand = jnp.concatenate([sv[0][a] + s2 for a in range(PEER_TOPK)], axis=0)
        cidx = jnp.concatenate([si[0][a] * PEER_KEYS + i2 for a in range(PEER_TOPK)], axis=0)
        tops, tidx = [], []
        for _ in range(PEER_TOPK):
            m, p = _top1(cand, crow, nc)
            hit = crow == p
            tops.append(m)
            tidx.append(jnp.max(jnp.where(hit, cidx, -1), axis=0, keepdims=True))
            cand = jnp.where(hit, -jnp.inf, cand)
        ts = jnp.concatenate(tops, axis=0)
        e = jnp.exp(ts - tops[0])
        gate = e / jnp.sum(e, axis=0, keepdims=True)
        rows = slice(h * PEER_TOPK, (h + 1) * PEER_TOPK)
        idx_ref[rows, :] = jnp.concatenate(tidx, axis=0)
        gate_ref[rows, :] = gate


def _route(xn, wq_bf, keys_bf):
    T = xn.shape[0]
    return pl.pallas_call(
        _route_kernel,
        grid=(T // RT_TM,),
        in_specs=[
            pl.BlockSpec((RT_TM, D_MODEL), lambda i: (i, 0)),
            pl.BlockSpec((D_MODEL, D_MODEL), lambda i: (0, 0)),
            pl.BlockSpec((2 * PEER_HEADS, PEER_KEYS, PEER_KEYS), lambda i: (0, 0, 0)),
        ],
        out_specs=[pl.BlockSpec((PEER_SLOTS, RT_TM), lambda i: (0, i)),
                   pl.BlockSpec((PEER_SLOTS, RT_TM), lambda i: (0, i))],
        out_shape=[jax.ShapeDtypeStruct((PEER_SLOTS, T), jnp.int32),
                   jax.ShapeDtypeStruct((PEER_SLOTS, T), F32)],
        compiler_params=_cparams(("arbitrary",)),
        name="route",
    )(xn, wq_bf, keys_bf)


PE_TB = 128
PE_NSLOT = 8
PE_GROUP = 8


def _peer_kernel(idx_ref, gate_ref, xn_ref, x1_ref, fg_ref, uv_hbm, y_ref, buf, sem):
    def issue(t, slot):
        def one(k, carry):
            e = idx_ref[t, k]
            pltpu.make_async_copy(uv_hbm.at[e], buf.at[slot, k], sem.at[slot]).start()
            return carry
        lax.fori_loop(0, PEER_SLOTS, one, 0)

    def wait(slot):
        pltpu.make_async_copy(uv_hbm.at[pl.ds(0, PEER_SLOTS)], buf.at[slot], sem.at[slot]).wait()

    for t0 in range(PE_NSLOT - 1):
        issue(t0, t0)

    gates = gate_ref[...]
    tok_lane = lax.broadcasted_iota(jnp.int32, gates.shape, 1)
    fg = fg_ref[...]

    def group(t8, carry):
        base = pl.multiple_of(t8 * PE_GROUP, PE_GROUP)
        xn8 = xn_ref[pl.ds(base, PE_GROUP), :]
        x18 = x1_ref[pl.ds(base, PE_GROUP), :]
        rows = []
        for j in range(PE_GROUP):
            t = base + j
            slot = j % PE_NSLOT
            ahead = t + PE_NSLOT - 1

            @pl.when(ahead < PE_TB)
            def _():
                issue(ahead, (j + PE_NSLOT - 1) % PE_NSLOT)

            wait(slot)
            u = buf[slot, :, :D_MODEL]
            hcol = jnp.sum(u * xn8[j:j + 1, :], axis=-1, keepdims=True)
            gcol = jnp.sum(jnp.where(tok_lane == t, gates, 0.0), axis=-1, keepdims=True)
            wcol = gcol * (0.5 * hcol * (1.0 + lax.erf(hcol * (2.0 ** -0.5))))
            v = buf[slot, :, D_MODEL:]
            rows.append(jnp.sum(wcol * v, axis=0, keepdims=True))
        x2 = x18 + jnp.concatenate(rows, axis=0)
        ms = jnp.mean(x2 * x2, axis=-1, keepdims=True)
        y_ref[pl.ds(base, PE_GROUP), :] = x2 * lax.rsqrt(ms + EPS) * fg
        return carry

    lax.fori_loop(0, PE_TB // PE_GROUP, group, 0)


def _peer(idx, gate_t, xn, x1, fg, uv):
    T = xn.shape[0]
    return pl.pallas_call(
        _peer_kernel,
        grid=(T // PE_TB,),
        in_specs=[
            pl.BlockSpec((PE_TB, PEER_SLOTS), lambda i: (i, 0), memory_space=pltpu.SMEM),
            pl.BlockSpec((PEER_SLOTS, PE_TB), lambda i: (0, i)),
            pl.BlockSpec((PE_TB, D_MODEL), lambda i: (i, 0)),
            pl.BlockSpec((PE_TB, D_MODEL), lambda i: (i, 0)),
            pl.BlockSpec((1, D_MODEL), lambda i: (0, 0)),
            pl.BlockSpec(memory_space=pl.ANY),
        ],
        out_specs=pl.BlockSpec((PE_TB, D_MODEL), lambda i: (i, 0)),
        out_shape=jax.ShapeDtypeStruct((T, D_MODEL), F32),
        scratch_shapes=[pltpu.VMEM((PE_NSLOT, PEER_SLOTS, 2 * D_MODEL), F32),
                        pltpu.SemaphoreType.DMA((PE_NSLOT,))],
        compiler_params=_cparams(("arbitrary",)),
        name="peer",
    )(idx, gate_t, xn, x1, fg, uv)


def _rope_tables(S):
    half = ATTN_HEAD_DIM // 2
    inv = 1.0 / (ROPE_THETA ** (jnp.arange(half, dtype=F32) / half))
    ang = jnp.arange(S, dtype=F32)[:, None] * inv[None, :]
    reps = LANES // half
    return jnp.tile(jnp.cos(ang), (1, reps)), jnp.tile(jnp.sin(ang), (1, reps))


def _trunk(x, p):
    S = x.shape[0]
    cos, sin = _rope_tables(S)
    z = _in_proj(x, p["norm1_g"], p["w_in"], cos, sin)
    a = _attention(z, p["lam_vecs"], p["subln_g"])
    x1, xn = _mix(x, z, a, p["pool_w"], p["pool_b"], p["pool_scale"], p["w_o"], p["norm2_g"])
    idx_t, gate_t = _route(xn, p["peer_wq"], p["peer_keys"])
    return _peer(idx_t.T, gate_t, xn, x1, p["final_norm_g"], p["uv"])


def kernel(x_prompt, x_sample, norm1_g, w_in, pool_w, pool_b, pool_scale, lambda_q1, lambda_k1,
           lambda_q2, lambda_k2, subln_g, w_o, norm2_g, peer_wq, peer_keys, peer_u, peer_v, final_norm_g):
    l = 0
    p = {
        "norm1_g": norm1_g[l][None, :],
        "w_in": w_in[l].astype(BF16),
        "pool_w": pool_w[l].astype(BF16),
        "pool_b": pool_b[l].reshape(1, POOL_WIDTH),
        "pool_scale": pool_scale[l][None, :],
        "lam_vecs": jnp.stack([lambda_q1[l], lambda_k1[l], lambda_q2[l], lambda_k2[l]]),
        "subln_g": subln_g[l][None, :],
        "w_o": w_o[l].astype(BF16),
        "norm2_g": norm2_g[l][None, :],
        "peer_wq": peer_wq[l].astype(BF16),
        "peer_keys": peer_keys[l].reshape(2 * PEER_HEADS, PEER_KEYS, PEER_KEYS).astype(BF16),
        "uv": jnp.concatenate([peer_u[l], peer_v[l]], axis=1),
        "final_norm_g": final_norm_g[None, :],
    }
    y_prompt = _trunk(x_prompt[0], p)[None]
    y_sample = _trunk(x_sample[0], p)[None]
    return (y_prompt, y_sample)
```

```python
import functools
import math

import jax
import jax.numpy as jnp
from jax import lax
from jax.experimental import pallas as pl
from jax.experimental.pallas import tpu as pltpu

F32 = jnp.float32
BF16 = jnp.bfloat16

D_MODEL = 2048
POOL_WIDTH = 1024
POOL_GROUPS = 4
POOL_GROUP_WIDTH = 256
POOL_WINDOWS = (2, 4, 8, 16)
POOL_HALO = 8
ATTN_HEADS = 8
ATTN_HEAD_DIM = 64
ATTN_V_DIM = 128
QK_WIDTH = 1024
ATTN_WIDTH = 1024
IN_PROJ_WIDTH = 4096
ROPE_THETA = 10000.0
PEER_HEADS = 8
PEER_KEYS = 128
PEER_EXPERTS = PEER_KEYS * PEER_KEYS
PEER_TOPK = 16
PEER_SLOTS = PEER_HEADS * PEER_TOPK
EPS = 1e-6
LAMBDA_INIT = 0.8 - 0.6 * math.exp(-0.3 * 0)
LOG2E = math.log2(math.e)

LANES = 128
VMEM_LIMIT = 56 * 1024 * 1024


def _cparams(sem):
    return pltpu.CompilerParams(dimension_semantics=sem, vmem_limit_bytes=VMEM_LIMIT)


IN_TM = 512
IN_TN = 512


def _inproj_kernel(x_ref, g_ref, w_ref, cos_ref, sin_ref, z_ref, h_scr):
    j = pl.program_id(1)

    @pl.when(j == 0)
    def _():
        x = x_ref[...]
        ms = jnp.mean(x * x, axis=-1, keepdims=True)
        h_scr[...] = (x * lax.rsqrt(ms + EPS) * g_ref[...]).astype(BF16)

    acc = jnp.dot(h_scr[...], w_ref[...], preferred_element_type=F32)
    q_lo, k_lo, v_lo = (POOL_WIDTH // IN_TN, (POOL_WIDTH + QK_WIDTH) // IN_TN,
                        (POOL_WIDTH + 2 * QK_WIDTH) // IN_TN)
    is_rope = jnp.logical_and(j >= q_lo, j < v_lo)

    @pl.when(is_rope)
    def _():
        cos = cos_ref[...]
        sin = sin_ref[...]
        scale = jnp.where(j < k_lo, ATTN_HEAD_DIM ** -0.5 * LOG2E, 1.0).astype(F32)
        lane = lax.broadcasted_iota(jnp.int32, cos.shape, 1)
        lower = (lane % ATTN_HEAD_DIM) < (ATTN_HEAD_DIM // 2)
        for c in range(IN_TN // LANES):
            xc = acc[:, c * LANES:(c + 1) * LANES]
            up = pltpu.roll(xc, LANES - ATTN_HEAD_DIM // 2, 1)
            dn = pltpu.roll(xc, ATTN_HEAD_DIM // 2, 1)
            rot = jnp.where(lower, -up, dn)
            z_ref[:, c * LANES:(c + 1) * LANES] = ((xc * cos + rot * sin) * scale).astype(BF16)

    @pl.when(jnp.logical_not(is_rope))
    def _():
        z_ref[...] = acc.astype(BF16)


def _in_proj(x, g, w_bf, cos, sin):
    S = x.shape[0]
    return pl.pallas_call(
        _inproj_kernel,
        grid=(S // IN_TM, IN_PROJ_WIDTH // IN_TN),
        in_specs=[
            pl.BlockSpec((IN_TM, D_MODEL), lambda i, j: (i, 0)),
            pl.BlockSpec((1, D_MODEL), lambda i, j: (0, 0)),
            pl.BlockSpec((D_MODEL, IN_TN), lambda i, j: (0, j)),
            pl.BlockSpec((IN_TM, LANES), lambda i, j: (i, 0)),
            pl.BlockSpec((IN_TM, LANES), lambda i, j: (i, 0)),
        ],
        out_specs=pl.BlockSpec((IN_TM, IN_TN), lambda i, j: (i, j)),
        out_shape=jax.ShapeDtypeStruct((S, IN_PROJ_WIDTH), BF16),
        scratch_shapes=[pltpu.VMEM((IN_TM, D_MODEL), BF16)],
        compiler_params=_cparams(("arbitrary", "arbitrary")),
        name="in_proj",
    )(x, g, w_bf, cos, sin)


AT_TQ = 256
AT_TK = 512


def _attn_kernel(q_ref, k_ref, v_ref, lam_ref, g_ref, o_ref,
                 qs_scr, s_scr, p_scr, alpha_scr, m_scr, acc_scr):
    n_chunks = k_ref.shape[0] // AT_TK
    q = q_ref[...]
    lane = lax.broadcasted_iota(jnp.int32, q.shape, 1)
    zero = jnp.zeros_like(q)
    qs_scr[...] = jnp.concatenate([jnp.where(lane < ATTN_HEAD_DIM, q, zero),
                                   jnp.where(lane >= ATTN_HEAD_DIM, q, zero)], axis=0)
    m_scr[...] = jnp.full(m_scr.shape, -jnp.inf, F32)
    acc_scr[...] = jnp.zeros(acc_scr.shape, F32)
    p_scr[1] = jnp.zeros(p_scr.shape[1:], BF16)
    alpha_scr[1] = jnp.ones(alpha_scr.shape[1:], F32)
    ones = jnp.ones((AT_TK, LANES), BF16)

    def scores(c):
        off = pl.multiple_of(c * AT_TK, AT_TK)
        return lax.dot_general(qs_scr[...], k_ref[pl.ds(off, AT_TK), :], (((1,), (1,)), ((), ())),
                               preferred_element_type=F32)

    def accumulate(c, b):
        off = pl.multiple_of(c * AT_TK, AT_TK)
        v_aug = jnp.concatenate([v_ref[pl.ds(off, AT_TK), :], ones], axis=1)
        pv = jnp.dot(p_scr[b], v_aug, preferred_element_type=F32)
        acc_scr[...] = jnp.tile(alpha_scr[b], (1, 2)) * acc_scr[...] + pv

    def stage(c, b):
        s_scr[1 - b] = scores(jnp.minimum(c + 1, n_chunks - 1))
        accumulate(jnp.maximum(c - 1, 0), 1 - b)
        s = s_scr[b]
        m_old = m_scr[...]
        m_new = jnp.maximum(m_old, jnp.max(s, axis=-1, keepdims=True))
        alpha_scr[b] = jnp.exp2(m_old - m_new)
        p_scr[b] = jnp.exp2(s - jnp.tile(m_new, (1, AT_TK // LANES))).astype(BF16)
        m_scr[...] = m_new

    s_scr[0] = scores(0)

    def body(c2, carry):
        stage(2 * c2, 0)
        stage(2 * c2 + 1, 1)
        return carry

    lax.fori_loop(0, n_chunks // 2, body, 0)
    accumulate(n_chunks - 1, 1)

    lv = lam_ref[...]
    lam = (jnp.exp(jnp.sum(lv[0:1] * lv[1:2], axis=-1, keepdims=True))
           - jnp.exp(jnp.sum(lv[2:3] * lv[3:4], axis=-1, keepdims=True)) + LAMBDA_INIT)
    acc = acc_scr[...]
    o = acc[:, :ATTN_V_DIM] / acc[:, ATTN_V_DIM:]
    d = o[:AT_TQ] - lam * o[AT_TQ:]
    ms = jnp.mean(d * d, axis=-1, keepdims=True)
    y = d * lax.rsqrt(ms + EPS) * g_ref[...] * (1.0 - LAMBDA_INIT)
    o_ref[...] = y.astype(BF16)


def _attention(z, lam_vecs, subln_g):
    S = z.shape[0]
    qb, kb, vb = POOL_WIDTH // LANES, (POOL_WIDTH + QK_WIDTH) // LANES, (POOL_WIDTH + 2 * QK_WIDTH) // LANES
    return pl.pallas_call(
        _attn_kernel,
        grid=(ATTN_HEADS, S // AT_TQ),
        in_specs=[
            pl.BlockSpec((AT_TQ, LANES), lambda h, i: (i, qb + h)),
            pl.BlockSpec((S, LANES), lambda h, i: (0, kb + h)),
            pl.BlockSpec((S, LANES), lambda h, i: (0, vb + h)),
            pl.BlockSpec((4, ATTN_HEAD_DIM), lambda h, i: (0, 0)),
            pl.BlockSpec((1, ATTN_V_DIM), lambda h, i: (0, 0)),
        ],
        out_specs=pl.BlockSpec((AT_TQ, LANES), lambda h, i: (i, h)),
        out_shape=jax.ShapeDtypeStruct((S, ATTN_WIDTH), BF16),
        scratch_shapes=[pltpu.VMEM((2 * AT_TQ, LANES), BF16),
                        pltpu.VMEM((2, 2 * AT_TQ, AT_TK), F32),
                        pltpu.VMEM((2, 2 * AT_TQ, AT_TK), BF16),
                        pltpu.VMEM((2, 2 * AT_TQ, LANES), F32),
                        pltpu.VMEM((2 * AT_TQ, LANES), F32),
                        pltpu.VMEM((2 * AT_TQ, 2 * ATTN_V_DIM), F32)],
        compiler_params=_cparams(("arbitrary", "arbitrary")),
        name="attn",
    )(z, z, z, lam_vecs, subln_g)


MX_TM = 256
MX_HB = 16


def _mix_kernel(x_ref, zp_ref, zm_ref, zn_ref, a_ref, pw_ref, pb_ref, ps_ref, wo_ref, g2_ref,
                x1_ref, xn_ref, *, seq_len):
    i = pl.program_id(0)
    r0 = i * MX_TM
    main = zm_ref[...].astype(F32)
    ext = jnp.concatenate([zp_ref[...].astype(F32)[MX_HB - POOL_HALO:], main,
                           zn_ref[...].astype(F32)[:POOL_HALO]], axis=0)
    ext_bf = ext.astype(BF16)
    ne = MX_TM + 2 * POOL_HALO
    t = r0 + lax.broadcasted_iota(jnp.int32, (MX_TM, ne), 0)
    r = r0 - POOL_HALO + lax.broadcasted_iota(jnp.int32, (MX_TM, ne), 1)
    tcol = r0 + lax.broadcasted_iota(jnp.int32, (MX_TM, 1), 0)
    acc = jnp.dot(a_ref[...], wo_ref[POOL_WIDTH:, :], preferred_element_type=F32)
    for gi, w in enumerate(POOL_WINDOWS):
        lo = jnp.maximum(t - w // 2, 0)
        hi = jnp.minimum(t + w // 2, seq_len)
        band = jnp.where(jnp.logical_and(r >= lo, r < hi), 1.0, 0.0).astype(BF16)
        cnt = (jnp.minimum(tcol + w // 2, seq_len) - jnp.maximum(tcol - w // 2, 0)).astype(F32)
        cs = slice(gi * POOL_GROUP_WIDTH, (gi + 1) * POOL_GROUP_WIDTH)
        wsum = jnp.dot(band, ext_bf[:, cs], preferred_element_type=F32)
        pooled = wsum / cnt - main[:, cs]
        y = jnp.dot(pooled.astype(BF16), pw_ref[gi], preferred_element_type=F32)
        y = (y + pb_ref[:, cs]) * ps_ref[:, cs]
        acc = acc + jnp.dot(y.astype(BF16), wo_ref[cs, :], preferred_element_type=F32)
    x1 = x_ref[...] + acc
    x1_ref[...] = x1
    ms = jnp.mean(x1 * x1, axis=-1, keepdims=True)
    xn_ref[...] = x1 * lax.rsqrt(ms + EPS) * g2_ref[...]


def _mix(x, z, a, pool_w_bf, pool_b, pool_scale, w_o_bf, norm2_g):
    S = x.shape[0]
    hb = MX_TM // MX_HB
    nhb = S // MX_HB
    return pl.pallas_call(
        functools.partial(_mix_kernel, seq_len=S),
        grid=(S // MX_TM,),
        in_specs=[
            pl.BlockSpec((MX_TM, D_MODEL), lambda i: (i, 0)),
            pl.BlockSpec((MX_HB, POOL_WIDTH), lambda i: (jnp.maximum(i * hb - 1, 0), 0)),
            pl.BlockSpec((MX_TM, POOL_WIDTH), lambda i: (i, 0)),
            pl.BlockSpec((MX_HB, POOL_WIDTH), lambda i: (jnp.minimum((i + 1) * hb, nhb - 1), 0)),
            pl.BlockSpec((MX_TM, ATTN_WIDTH), lambda i: (i, 0)),
            pl.BlockSpec((POOL_GROUPS, POOL_GROUP_WIDTH, POOL_GROUP_WIDTH), lambda i: (0, 0, 0)),
            pl.BlockSpec((1, POOL_WIDTH), lambda i: (0, 0)),
            pl.BlockSpec((1, POOL_WIDTH), lambda i: (0, 0)),
            pl.BlockSpec((D_MODEL, D_MODEL), lambda i: (0, 0)),
            pl.BlockSpec((1, D_MODEL), lambda i: (0, 0)),
        ],
        out_specs=[pl.BlockSpec((MX_TM, D_MODEL), lambda i: (i, 0)),
                   pl.BlockSpec((MX_TM, D_MODEL), lambda i: (i, 0))],
        out_shape=[jax.ShapeDtypeStruct((S, D_MODEL), F32), jax.ShapeDtypeStruct((S, D_MODEL), F32)],
        compiler_params=_cparams(("arbitrary",)),
        name="mix",
    )(x, z, z, z, a, pool_w_bf, pool_b, pool_scale, w_o_bf, norm2_g)


RT_TM = 256


def _top1(s, pos, sentinel):
    m = jnp.max(s, axis=0, keepdims=True)
    p = jnp.min(jnp.where(s == m, pos, sentinel), axis=0, keepdims=True)
    return m, p


def _route_kernel(xn_ref, wq_ref, keys_ref, idx_ref, gate_ref):
    q = jnp.dot(xn_ref[...].astype(BF16), wq_ref[...], preferred_element_type=F32)
    krow = lax.broadcasted_iota(jnp.int32, (PEER_KEYS, RT_TM), 0)
    nc = PEER_TOPK * PEER_TOPK
    crow = lax.broadcasted_iota(jnp.int32, (nc, RT_TM), 0)
    for h in range(PEER_HEADS):
        sv, si = [], []
        for c in range(2):
            hc = 2 * h + c
            qc = q[:, hc * PEER_KEYS:(hc + 1) * PEER_KEYS].astype(BF16)
            s = lax.dot_general(keys_ref[hc], qc, (((1,), (1,)), ((), ())),
                                preferred_element_type=F32)
            vals, ids = [], []
            for _ in range(PEER_TOPK):
                m, p = _top1(s, krow, PEER_KEYS)
                vals.append(m)
                ids.append(p)
                s = jnp.where(krow == p, -jnp.inf, s)
            sv.append(vals)
            si.append(ids)
        s2 = jnp.concatenate(sv[1], axis=0)
        i2 = jnp.concatenate(si[1], axis=0)
        cand = jnp.concatenate([sv[0][a] + s2 for a in range(PEER_TOPK)], axis=0)
        cidx = jnp.concatenate([si[0][a] * PEER_KEYS + i2 for a in range(PEER_TOPK)], axis=0)
        tops, tidx = [], []
        for _ in range(PEER_TOPK):
            m, p = _top1(cand, crow, nc)
            hit = crow == p
            tops.append(m)
            tidx.append(jnp.max(jnp.where(hit, cidx, -1), axis=0, keepdims=True))
            cand = jnp.where(hit, -jnp.inf, cand)
        ts = jnp.concatenate(tops, axis=0)
        e = jnp.exp(ts - tops[0])
        gate = e / jnp.sum(e, axis=0, keepdims=True)
        rows = slice(h * PEER_TOPK, (h + 1) * PEER_TOPK)
        idx_ref[rows, :] = jnp.concatenate(tidx, axis=0)
        gate_ref[rows, :] = gate


def _route(xn, wq_bf, keys_bf):
    T = xn.shape[0]
    return pl.pallas_call(
        _route_kernel,
        grid=(T // RT_TM,),
        in_specs=[
            pl.BlockSpec((RT_TM, D_MODEL), lambda i: (i, 0)),
            pl.BlockSpec((D_MODEL, D_MODEL), lambda i: (0, 0)),
            pl.BlockSpec((2 * PEER_HEADS, PEER_KEYS, PEER_KEYS), lambda i: (0, 0, 0)),
        ],
        out_specs=[pl.BlockSpec((PEER_SLOTS, RT_TM), lambda i: (0, i)),
                   pl.BlockSpec((PEER_SLOTS, RT_TM), lambda i: (0, i))],
        out_shape=[jax.ShapeDtypeStruct((PEER_SLOTS, T), jnp.int32),
                   jax.ShapeDtypeStruct((PEER_SLOTS, T), F32)],
        compiler_params=_cparams(("arbitrary",)),
        name="route",
    )(xn, wq_bf, keys_bf)


PE_TB = 128
PE_NSLOT = 8
PE_GROUP = 8
PE_BATCH = PE_NSLOT // 2


def _peer_kernel(idx_ref, gate_ref, xn_ref, x1_ref, fg_ref, uv_hbm, y_ref, buf, sem):
    def issue_batch(t_first):
        def one_token(b, carry):
            t = t_first + b
            slot = t % PE_NSLOT
            for k in range(PEER_SLOTS):
                pltpu.make_async_copy(uv_hbm.at[idx_ref[t, k]], buf.at[slot, k], sem.at[slot]).start()
            return carry
        lax.fori_loop(0, PE_BATCH, one_token, 0)

    def wait(slot):
        pltpu.make_async_copy(uv_hbm.at[pl.ds(0, PEER_SLOTS)], buf.at[slot], sem.at[slot]).wait()

    issue_batch(0)

    gates = gate_ref[...]
    tok_lane = lax.broadcasted_iota(jnp.int32, gates.shape, 1)
    fg = fg_ref[...]

    def group(t8, carry):
        base = pl.multiple_of(t8 * PE_GROUP, PE_GROUP)
        xn8 = xn_ref[pl.ds(base, PE_GROUP), :]
        x18 = x1_ref[pl.ds(base, PE_GROUP), :]
        rows = []
        for j in range(PE_GROUP):
            t = base + j
            if j % PE_BATCH == 0:
                @pl.when(t + PE_BATCH < PE_TB)
                def _():
                    issue_batch(t + PE_BATCH)
            slot = j % PE_NSLOT
            wait(slot)
            u = buf[slot, :, :D_MODEL]
            hcol = jnp.sum(u * xn8[j:j + 1, :], axis=-1, keepdims=True)
            gcol = jnp.sum(jnp.where(tok_lane == t, gates, 0.0), axis=-1, keepdims=True)
            wcol = gcol * (0.5 * hcol * (1.0 + lax.erf(hcol * (2.0 ** -0.5))))
            v = buf[slot, :, D_MODEL:]
            rows.append(jnp.sum(wcol * v, axis=0, keepdims=True))
        x2 = x18 + jnp.concatenate(rows, axis=0)
        ms = jnp.mean(x2 * x2, axis=-1, keepdims=True)
        y_ref[pl.ds(base, PE_GROUP), :] = x2 * lax.rsqrt(ms + EPS) * fg
        return carry

    lax.fori_loop(0, PE_TB // PE_GROUP, group, 0)


def _peer(idx, gate_t, xn, x1, fg, uv):
    T = xn.shape[0]
    return pl.pallas_call(
        _peer_kernel,
        grid=(T // PE_TB,),
        in_specs=[
            pl.BlockSpec((PE_TB, PEER_SLOTS), lambda i: (i, 0), memory_space=pltpu.SMEM),
            pl.BlockSpec((PEER_SLOTS, PE_TB), lambda i: (0, i)),
            pl.BlockSpec((PE_TB, D_MODEL), lambda i: (i, 0)),
            pl.BlockSpec((PE_TB, D_MODEL), lambda i: (i, 0)),
            pl.BlockSpec((1, D_MODEL), lambda i: (0, 0)),
            pl.BlockSpec(memory_space=pl.ANY),
        ],
        out_specs=pl.BlockSpec((PE_TB, D_MODEL), lambda i: (i, 0)),
        out_shape=jax.ShapeDtypeStruct((T, D_MODEL), F32),
        scratch_shapes=[pltpu.VMEM((PE_NSLOT, PEER_SLOTS, 2 * D_MODEL), F32),
                        pltpu.SemaphoreType.DMA((PE_NSLOT,))],
        compiler_params=_cparams(("arbitrary",)),
        name="peer",
    )(idx, gate_t, xn, x1, fg, uv)


def _rope_tables(S):
    half = ATTN_HEAD_DIM // 2
    inv = 1.0 / (ROPE_THETA ** (jnp.arange(half, dtype=F32) / half))
    ang = jnp.arange(S, dtype=F32)[:, None] * inv[None, :]
    reps = LANES // half
    return jnp.tile(jnp.cos(ang), (1, reps)), jnp.tile(jnp.sin(ang), (1, reps))


def _trunk(x, p):
    S = x.shape[0]
    cos, sin = _rope_tables(S)
    z = _in_proj(x, p["norm1_g"], p["w_in"], cos, sin)
    a = _attention(z, p["lam_vecs"], p["subln_g"])
    x1, xn = _mix(x, z, a, p["pool_w"], p["pool_b"], p["pool_scale"], p["w_o"], p["norm2_g"])
    idx_t, gate_t = _route(xn, p["peer_wq"], p["peer_keys"])
    return _peer(idx_t.T, gate_t, xn, x1, p["final_norm_g"], p["uv"])


def kernel(x_prompt, x_sample, norm1_g, w_in, pool_w, pool_b, pool_scale, lambda_q1, lambda_k1,
           lambda_q2, lambda_k2, subln_g, w_o, norm2_g, peer_wq, peer_keys, peer_u, peer_v, final_norm_g):
    l = 0
    p = {
        "norm1_g": norm1_g[l][None, :],
        "w_in": w_in[l].astype(BF16),
        "pool_w": pool_w[l].astype(BF16),
        "pool_b": pool_b[l].reshape(1, POOL_WIDTH),
        "pool_scale": pool_scale[l][None, :],
        "lam_vecs": jnp.stack([lambda_q1[l], lambda_k1[l], lambda_q2[l], lambda_k2[l]]),
        "subln_g": subln_g[l][None, :],
        "w_o": w_o[l].astype(BF16),
        "norm2_g": norm2_g[l][None, :],
        "peer_wq": peer_wq[l].astype(BF16),
        "peer_keys": peer_keys[l].reshape(2 * PEER_HEADS, PEER_KEYS, PEER_KEYS).astype(BF16),
        "uv": jnp.concatenate([peer_u[l], peer_v[l]], axis=1),
        "final_norm_g": final_norm_g[None, :],
    }
    y_prompt = _trunk(x_prompt[0], p)[None]
    y_sample = _trunk(x_sample[0], p)[None]
    return (y_prompt, y_sample)
```

```python
import functools
import math

import jax
import jax.numpy as jnp
from jax import lax
from jax.experimental import pallas as pl
from jax.experimental.pallas import tpu as pltpu

F32 = jnp.float32
BF16 = jnp.bfloat16

D_MODEL = 2048
POOL_WIDTH = 1024
POOL_GROUPS = 4
POOL_GROUP_WIDTH = 256
POOL_WINDOWS = (2, 4, 8, 16)
POOL_HALO = 8
ATTN_HEADS = 8
ATTN_HEAD_DIM = 64
ATTN_V_DIM = 128
QK_WIDTH = 1024
ATTN_WIDTH = 1024
IN_PROJ_WIDTH = 4096
ROPE_THETA = 10000.0
PEER_HEADS = 8
PEER_KEYS = 128
PEER_EXPERTS = PEER_KEYS * PEER_KEYS
PEER_TOPK = 16
PEER_SLOTS = PEER_HEADS * PEER_TOPK
EPS = 1e-6
LAMBDA_INIT = 0.8 - 0.6 * math.exp(-0.3 * 0)
LOG2E = math.log2(math.e)

LANES = 128
VMEM_LIMIT = 56 * 1024 * 1024


def _cparams(sem):
    return pltpu.CompilerParams(dimension_semantics=sem, vmem_limit_bytes=VMEM_LIMIT)


IN_TM = 512
IN_TN = 512


def _inproj_kernel(x_ref, g_ref, w_ref, cos_ref, sin_ref, z_ref, h_scr):
    j = pl.program_id(1)

    @pl.when(j == 0)
    def _():
        x = x_ref[...]
        ms = jnp.mean(x * x, axis=-1, keepdims=True)
        h_scr[...] = (x * lax.rsqrt(ms + EPS) * g_ref[...]).astype(BF16)

    acc = jnp.dot(h_scr[...], w_ref[...], preferred_element_type=F32)
    q_lo, k_lo, v_lo = (POOL_WIDTH // IN_TN, (POOL_WIDTH + QK_WIDTH) // IN_TN,
                        (POOL_WIDTH + 2 * QK_WIDTH) // IN_TN)
    is_rope = jnp.logical_and(j >= q_lo, j < v_lo)

    @pl.when(is_rope)
    def _():
        cos = cos_ref[...]
        sin = sin_ref[...]
        scale = jnp.where(j < k_lo, ATTN_HEAD_DIM ** -0.5 * LOG2E, 1.0).astype(F32)
        lane = lax.broadcasted_iota(jnp.int32, cos.shape, 1)
        lower = (lane % ATTN_HEAD_DIM) < (ATTN_HEAD_DIM // 2)
        for c in range(IN_TN // LANES):
            xc = acc[:, c * LANES:(c + 1) * LANES]
            up = pltpu.roll(xc, LANES - ATTN_HEAD_DIM // 2, 1)
            dn = pltpu.roll(xc, ATTN_HEAD_DIM // 2, 1)
            rot = jnp.where(lower, -up, dn)
            z_ref[:, c * LANES:(c + 1) * LANES] = ((xc * cos + rot * sin) * scale).astype(BF16)

    @pl.when(jnp.logical_not(is_rope))
    def _():
        z_ref[...] = acc.astype(BF16)


def _in_proj(x, g, w_bf, cos, sin):
    S = x.shape[0]
    return pl.pallas_call(
        _inproj_kernel,
        grid=(S // IN_TM, IN_PROJ_WIDTH // IN_TN),
        in_specs=[
            pl.BlockSpec((IN_TM, D_MODEL), lambda i, j: (i, 0)),
            pl.BlockSpec((1, D_MODEL), lambda i, j: (0, 0)),
            pl.BlockSpec((D_MODEL, IN_TN), lambda i, j: (0, j)),
            pl.BlockSpec((IN_TM, LANES), lambda i, j: (i, 0)),
            pl.BlockSpec((IN_TM, LANES), lambda i, j: (i, 0)),
        ],
        out_specs=pl.BlockSpec((IN_TM, IN_TN), lambda i, j: (i, j)),
        out_shape=jax.ShapeDtypeStruct((S, IN_PROJ_WIDTH), BF16),
        scratch_shapes=[pltpu.VMEM((IN_TM, D_MODEL), BF16)],
        compiler_params=_cparams(("arbitrary", "arbitrary")),
        name="in_proj",
    )(x, g, w_bf, cos, sin)


AT_TQ = 256
AT_TK = 512


def _attn_kernel(q_ref, k_ref, v_ref, lam_ref, g_ref, o_ref,
                 qs_scr, s_scr, p_scr, alpha_scr, m_scr, acc_scr):
    n_chunks = k_ref.shape[0] // AT_TK
    assert n_chunks % 2 == 0, "the pipeline alternates two buffer sets"
    q = q_ref[...]
    lane = lax.broadcasted_iota(jnp.int32, q.shape, 1)
    zero = jnp.zeros_like(q)
    qs_scr[...] = jnp.concatenate([jnp.where(lane < ATTN_HEAD_DIM, q, zero),
                                   jnp.where(lane >= ATTN_HEAD_DIM, q, zero)], axis=0)
    m_scr[...] = jnp.full(m_scr.shape, -jnp.inf, F32)
    acc_scr[...] = jnp.zeros(acc_scr.shape, F32)
    p_scr[1] = jnp.zeros(p_scr.shape[1:], BF16)
    alpha_scr[1] = jnp.ones(alpha_scr.shape[1:], F32)
    ones = jnp.ones((AT_TK, LANES), BF16)

    def scores(c):
        off = pl.multiple_of(c * AT_TK, AT_TK)
        return lax.dot_general(qs_scr[...], k_ref[pl.ds(off, AT_TK), :], (((1,), (1,)), ((), ())),
                               preferred_element_type=F32)

    def accumulate(c, b):
        off = pl.multiple_of(c * AT_TK, AT_TK)
        v_aug = jnp.concatenate([v_ref[pl.ds(off, AT_TK), :], ones], axis=1)
        pv = jnp.dot(p_scr[b], v_aug, preferred_element_type=F32)
        acc_scr[...] = jnp.tile(alpha_scr[b], (1, 2)) * acc_scr[...] + pv

    def stage(c, b):
        s_scr[1 - b] = scores(jnp.minimum(c + 1, n_chunks - 1))
        accumulate(jnp.maximum(c - 1, 0), 1 - b)
        s = s_scr[b]
        m_old = m_scr[...]
        m_new = jnp.maximum(m_old, jnp.max(s, axis=-1, keepdims=True))
        alpha_scr[b] = jnp.exp2(m_old - m_new)
        p_scr[b] = jnp.exp2(s - jnp.tile(m_new, (1, AT_TK // LANES))).astype(BF16)
        m_scr[...] = m_new

    s_scr[0] = scores(0)

    def body(c2, carry):
        stage(2 * c2, 0)
        stage(2 * c2 + 1, 1)
        return carry

    lax.fori_loop(0, n_chunks // 2, body, 0)
    accumulate(n_chunks - 1, 1)

    lv = lam_ref[...]
    lam = (jnp.exp(jnp.sum(lv[0:1] * lv[1:2], axis=-1, keepdims=True))
           - jnp.exp(jnp.sum(lv[2:3] * lv[3:4], axis=-1, keepdims=True)) + LAMBDA_INIT)
    acc = acc_scr[...]
    o = acc[:, :ATTN_V_DIM] / acc[:, ATTN_V_DIM:]
    d = o[:AT_TQ] - lam * o[AT_TQ:]
    ms = jnp.mean(d * d, axis=-1, keepdims=True)
    y = d * lax.rsqrt(ms + EPS) * g_ref[...] * (1.0 - LAMBDA_INIT)
    o_ref[...] = y.astype(BF16)


def _attention(z, lam_vecs, subln_g):
    S = z.shape[0]
    qb, kb, vb = POOL_WIDTH // LANES, (POOL_WIDTH + QK_WIDTH) // LANES, (POOL_WIDTH + 2 * QK_WIDTH) // LANES
    return pl.pallas_call(
        _attn_kernel,
        grid=(ATTN_HEADS, S // AT_TQ),
        in_specs=[
            pl.BlockSpec((AT_TQ, LANES), lambda h, i: (i, qb + h)),
            pl.BlockSpec((S, LANES), lambda h, i: (0, kb + h)),
            pl.BlockSpec((S, LANES), lambda h, i: (0, vb + h)),
            pl.BlockSpec((4, ATTN_HEAD_DIM), lambda h, i: (0, 0)),
            pl.BlockSpec((1, ATTN_V_DIM), lambda h, i: (0, 0)),
        ],
        out_specs=pl.BlockSpec((AT_TQ, LANES), lambda h, i: (i, h)),
        out_shape=jax.ShapeDtypeStruct((S, ATTN_WIDTH), BF16),
        scratch_shapes=[pltpu.VMEM((2 * AT_TQ, LANES), BF16),
                        pltpu.VMEM((2, 2 * AT_TQ, AT_TK), F32),
                        pltpu.VMEM((2, 2 * AT_TQ, AT_TK), BF16),
                        pltpu.VMEM((2, 2 * AT_TQ, LANES), F32),
                        pltpu.VMEM((2 * AT_TQ, LANES), F32),
                        pltpu.VMEM((2 * AT_TQ, 2 * ATTN_V_DIM), F32)],
        compiler_params=_cparams(("arbitrary", "arbitrary")),
        name="attn",
    )(z, z, z, lam_vecs, subln_g)


MX_TM = 256
MX_HB = 16


def _mix_kernel(x_ref, zp_ref, zm_ref, zn_ref, a_ref, pw_ref, pb_ref, ps_ref, wo_ref, g2_ref,
                x1_ref, xn_ref, *, seq_len):
    i = pl.program_id(0)
    r0 = i * MX_TM
    main = zm_ref[...].astype(F32)
    ext = jnp.concatenate([zp_ref[...].astype(F32)[MX_HB - POOL_HALO:], main,
                           zn_ref[...].astype(F32)[:POOL_HALO]], axis=0)
    ext_bf = ext.astype(BF16)
    ne = MX_TM + 2 * POOL_HALO
    t = r0 + lax.broadcasted_iota(jnp.int32, (MX_TM, ne), 0)
    r = r0 - POOL_HALO + lax.broadcasted_iota(jnp.int32, (MX_TM, ne), 1)
    tcol = r0 + lax.broadcasted_iota(jnp.int32, (MX_TM, 1), 0)
    acc = jnp.dot(a_ref[...], wo_ref[POOL_WIDTH:, :], preferred_element_type=F32)
    for gi, w in enumerate(POOL_WINDOWS):
        lo = jnp.maximum(t - w // 2, 0)
        hi = jnp.minimum(t + w // 2, seq_len)
        band = jnp.where(jnp.logical_and(r >= lo, r < hi), 1.0, 0.0).astype(BF16)
        cnt = (jnp.minimum(tcol + w // 2, seq_len) - jnp.maximum(tcol - w // 2, 0)).astype(F32)
        cs = slice(gi * POOL_GROUP_WIDTH, (gi + 1) * POOL_GROUP_WIDTH)
        wsum = jnp.dot(band, ext_bf[:, cs], preferred_element_type=F32)
        pooled = wsum / cnt - main[:, cs]
        y = jnp.dot(pooled.astype(BF16), pw_ref[gi], preferred_element_type=F32)
        y = (y + pb_ref[:, cs]) * ps_ref[:, cs]
        acc = acc + jnp.dot(y.astype(BF16), wo_ref[cs, :], preferred_element_type=F32)
    x1 = x_ref[...] + acc
    x1_ref[...] = x1
    ms = jnp.mean(x1 * x1, axis=-1, keepdims=True)
    xn_ref[...] = x1 * lax.rsqrt(ms + EPS) * g2_ref[...]


def _mix(x, z, a, pool_w_bf, pool_b, pool_scale, w_o_bf, norm2_g):
    S = x.shape[0]
    hb = MX_TM // MX_HB
    nhb = S // MX_HB
    return pl.pallas_call(
        functools.partial(_mix_kernel, seq_len=S),
        grid=(S // MX_TM,),
        in_specs=[
            pl.BlockSpec((MX_TM, D_MODEL), lambda i: (i, 0)),
            pl.BlockSpec((MX_HB, POOL_WIDTH), lambda i: (jnp.maximum(i * hb - 1, 0), 0)),
            pl.BlockSpec((MX_TM, POOL_WIDTH), lambda i: (i, 0)),
            pl.BlockSpec((MX_HB, POOL_WIDTH), lambda i: (jnp.minimum((i + 1) * hb, nhb - 1), 0)),
            pl.BlockSpec((MX_TM, ATTN_WIDTH), lambda i: (i, 0)),
            pl.BlockSpec((POOL_GROUPS, POOL_GROUP_WIDTH, POOL_GROUP_WIDTH), lambda i: (0, 0, 0)),
            pl.BlockSpec((1, POOL_WIDTH), lambda i: (0, 0)),
            pl.BlockSpec((1, POOL_WIDTH), lambda i: (0, 0)),
            pl.BlockSpec((D_MODEL, D_MODEL), lambda i: (0, 0)),
            pl.BlockSpec((1, D_MODEL), lambda i: (0, 0)),
        ],
        out_specs=[pl.BlockSpec((MX_TM, D_MODEL), lambda i: (i, 0)),
                   pl.BlockSpec((MX_TM, D_MODEL), lambda i: (i, 0))],
        out_shape=[jax.ShapeDtypeStruct((S, D_MODEL), F32), jax.ShapeDtypeStruct((S, D_MODEL), F32)],
        compiler_params=_cparams(("arbitrary",)),
        name="mix",
    )(x, z, z, z, a, pool_w_bf, pool_b, pool_scale, w_o_bf, norm2_g)


RT_TM = 256


def _top1(s, pos, sentinel):
    m = jnp.max(s, axis=0, keepdims=True)
    p = jnp.min(jnp.where(s == m, pos, sentinel), axis=0, keepdims=True)
    return m, p


def _route_kernel(xn_ref, wq_ref, keys_ref, idx_ref, gate_ref):
    q = jnp.dot(xn_ref[...].astype(BF16), wq_ref[...], preferred_element_type=F32)
    krow = lax.broadcasted_iota(jnp.int32, (PEER_KEYS, RT_TM), 0)
    nc = PEER_TOPK * PEER_TOPK
    half = PEER_TOPK // 2
    n_kept = PEER_TOPK + (half - 1) * half + half
    r = lax.broadcasted_iota(jnp.int32, (n_kept, RT_TM), 0)
    mid = r - PEER_TOPK
    cpos = jnp.where(r < PEER_TOPK, r,
                     jnp.where(r < n_kept - half, (1 + (mid >> (half.bit_length() - 1))) * PEER_TOPK + (mid & (half - 1)),
                               (half + r - (n_kept - half)) * PEER_TOPK))
    for h in range(PEER_HEADS):
        sv, si = [], []
        for c in range(2):
            hc = 2 * h + c
            qc = q[:, hc * PEER_KEYS:(hc + 1) * PEER_KEYS].astype(BF16)
            s = lax.dot_general(keys_ref[hc], qc, (((1,), (1,)), ((), ())),
                                preferred_element_type=F32)
            vals, ids = [], []
            for _ in range(PEER_TOPK):
                m, p = _top1(s, krow, PEER_KEYS)
                vals.append(m)
                ids.append(p)
                s = jnp.where(krow == p, -jnp.inf, s)
            sv.append(vals)
            si.append(ids)
        s2 = jnp.concatenate(sv[1], axis=0)
        i2 = jnp.concatenate(si[1], axis=0)
        s1_hi = jnp.concatenate(sv[0][half:], axis=0)
        i1_hi = jnp.concatenate(si[0][half:], axis=0)
        cand = jnp.concatenate([sv[0][0] + s2] + [sv[0][a] + s2[:half] for a in range(1, half)]
                               + [s1_hi + sv[1][0]], axis=0)
        cidx = jnp.concatenate([si[0][0] * PEER_KEYS + i2]
                               + [si[0][a] * PEER_KEYS + i2[:half] for a in range(1, half)]
                               + [i1_hi * PEER_KEYS + si[1][0]], axis=0)
        tops, tidx = [], []
        for _ in range(PEER_TOPK):
            m, p = _top1(cand, cpos, nc)
            hit = cpos == p
            tops.append(m)
            tidx.append(jnp.max(jnp.where(hit, cidx, -1), axis=0, keepdims=True))
            cand = jnp.where(hit, -jnp.inf, cand)
        ts = jnp.concatenate(tops, axis=0)
        e = jnp.exp(ts - tops[0])
        gate = e / jnp.sum(e, axis=0, keepdims=True)
        rows = slice(h * PEER_TOPK, (h + 1) * PEER_TOPK)
        idx_ref[rows, :] = jnp.concatenate(tidx, axis=0)
        gate_ref[rows, :] = gate


def _route(xn, wq_bf, keys_bf):
    T = xn.shape[0]
    return pl.pallas_call(
        _route_kernel,
        grid=(T // RT_TM,),
        in_specs=[
            pl.BlockSpec((RT_TM, D_MODEL), lambda i: (i, 0)),
            pl.BlockSpec((D_MODEL, D_MODEL), lambda i: (0, 0)),
            pl.BlockSpec((2 * PEER_HEADS, PEER_KEYS, PEER_KEYS), lambda i: (0, 0, 0)),
        ],
        out_specs=[pl.BlockSpec((PEER_SLOTS, RT_TM), lambda i: (0, i)),
                   pl.BlockSpec((PEER_SLOTS, RT_TM), lambda i: (0, i))],
        out_shape=[jax.ShapeDtypeStruct((PEER_SLOTS, T), jnp.int32),
                   jax.ShapeDtypeStruct((PEER_SLOTS, T), F32)],
        compiler_params=_cparams(("arbitrary",)),
        name="route",
    )(xn, wq_bf, keys_bf)


PE_TB = 128
PE_NSLOT = 8
PE_GROUP = 8
PE_BATCH = PE_NSLOT // 2
assert PE_GROUP == PE_NSLOT and PE_TB % PE_GROUP == 0


def _peer_kernel(idx_ref, gate_ref, xn_ref, x1_ref, fg_ref, uv_hbm, y_ref, buf, sem):
    def issue_batch(t_first, slot_first):
        for b in range(PE_BATCH):
            slot = (slot_first + b) % PE_NSLOT
            for k in range(PEER_SLOTS):
                pltpu.make_async_copy(uv_hbm.at[idx_ref[t_first + b, k]], buf.at[slot, k],
                                      sem.at[slot]).start()

    def wait(slot):
        pltpu.make_async_copy(uv_hbm.at[pl.ds(0, PEER_SLOTS)], buf.at[slot], sem.at[slot]).wait()

    issue_batch(0, 0)

    gates = gate_ref[...]
    tok_lane = lax.broadcasted_iota(jnp.int32, gates.shape, 1)
    fg = fg_ref[...]

    def group(t8, carry):
        base = pl.multiple_of(t8 * PE_GROUP, PE_GROUP)
        xn8 = xn_ref[pl.ds(base, PE_GROUP), :]
        x18 = x1_ref[pl.ds(base, PE_GROUP), :]
        rows = []
        for j in range(PE_GROUP):
            t = base + j
            if j % PE_BATCH == 0:
                @pl.when(t + PE_BATCH < PE_TB)
                def _():
                    issue_batch(t + PE_BATCH, j + PE_BATCH)
            slot = j % PE_NSLOT
            wait(slot)
            u = buf[slot, :, :D_MODEL]
            hcol = jnp.sum(u * xn8[j:j + 1, :], axis=-1, keepdims=True)
            gcol = jnp.sum(jnp.where(tok_lane == t, gates, 0.0), axis=-1, keepdims=True)
            wcol = gcol * (0.5 * hcol * (1.0 + lax.erf(hcol * (2.0 ** -0.5))))
            v = buf[slot, :, D_MODEL:]
            rows.append(jnp.sum(wcol * v, axis=0, keepdims=True))
        x2 = x18 + jnp.concatenate(rows, axis=0)
        ms = jnp.mean(x2 * x2, axis=-1, keepdims=True)
        y_ref[pl.ds(base, PE_GROUP), :] = x2 * lax.rsqrt(ms + EPS) * fg
        return carry

    lax.fori_loop(0, PE_TB // PE_GROUP, group, 0)


def _peer(idx, gate_t, xn, x1, fg, uv):
    T = xn.shape[0]
    return pl.pallas_call(
        _peer_kernel,
        grid=(T // PE_TB,),
        in_specs=[
            pl.BlockSpec((PE_TB, PEER_SLOTS), lambda i: (i, 0), memory_space=pltpu.SMEM),
            pl.BlockSpec((PEER_SLOTS, PE_TB), lambda i: (0, i)),
            pl.BlockSpec((PE_TB, D_MODEL), lambda i: (i, 0)),
            pl.BlockSpec((PE_TB, D_MODEL), lambda i: (i, 0)),
            pl.BlockSpec((1, D_MODEL), lambda i: (0, 0)),
            pl.BlockSpec(memory_space=pl.ANY),
        ],
        out_specs=pl.BlockSpec((PE_TB, D_MODEL), lambda i: (i, 0)),
        out_shape=jax.ShapeDtypeStruct((T, D_MODEL), F32),
        scratch_shapes=[pltpu.VMEM((PE_NSLOT, PEER_SLOTS, 2 * D_MODEL), F32),
                        pltpu.SemaphoreType.DMA((PE_NSLOT,))],
        compiler_params=_cparams(("arbitrary",)),
        name="peer",
    )(idx, gate_t, xn, x1, fg, uv)


def _rope_tables(S):
    half = ATTN_HEAD_DIM // 2
    inv = 1.0 / (ROPE_THETA ** (jnp.arange(half, dtype=F32) / half))
    ang = jnp.arange(S, dtype=F32)[:, None] * inv[None, :]
    reps = LANES // half
    return jnp.tile(jnp.cos(ang), (1, reps)), jnp.tile(jnp.sin(ang), (1, reps))


def _trunk(x, p):
    S = x.shape[0]
    cos, sin = _rope_tables(S)
    z = _in_proj(x, p["norm1_g"], p["w_in"], cos, sin)
    a = _attention(z, p["lam_vecs"], p["subln_g"])
    x1, xn = _mix(x, z, a, p["pool_w"], p["pool_b"], p["pool_scale"], p["w_o"], p["norm2_g"])
    idx_t, gate_t = _route(xn, p["peer_wq"], p["peer_keys"])
    return _peer(idx_t.T, gate_t, xn, x1, p["final_norm_g"], p["uv"])


def kernel(x_prompt, x_sample, norm1_g, w_in, pool_w, pool_b, pool_scale, lambda_q1, lambda_k1,
           lambda_q2, lambda_k2, subln_g, w_o, norm2_g, peer_wq, peer_keys, peer_u, peer_v, final_norm_g):
    l = 0
    p = {
        "norm1_g": norm1_g[l][None, :],
        "w_in": w_in[l].astype(BF16),
        "pool_w": pool_w[l].astype(BF16),
        "pool_b": pool_b[l].reshape(1, POOL_WIDTH),
        "pool_scale": pool_scale[l][None, :],
        "lam_vecs": jnp.stack([lambda_q1[l], lambda_k1[l], lambda_q2[l], lambda_k2[l]]),
        "subln_g": subln_g[l][None, :],
        "w_o": w_o[l].astype(BF16),
        "norm2_g": norm2_g[l][None, :],
        "peer_wq": peer_wq[l].astype(BF16),
        "peer_keys": peer_keys[l].reshape(2 * PEER_HEADS, PEER_KEYS, PEER_KEYS).astype(BF16),
        "uv": jnp.concatenate([peer_u[l], peer_v[l]], axis=1),
        "final_norm_g": final_norm_g[None, :],
    }
    y_prompt = _trunk(x_prompt[0], p)[None]
    y_sample = _trunk(x_sample[0], p)[None]
    return (y_prompt, y_sample)
```

```python
import functools
import math

import jax
import jax.numpy as jnp
from jax import lax
from jax.experimental import pallas as pl
from jax.experimental.pallas import tpu as pltpu

F32 = jnp.float32
BF16 = jnp.bfloat16

D_MODEL = 2048
POOL_WIDTH = 1024
POOL_GROUPS = 4
POOL_GROUP_WIDTH = 256
POOL_WINDOWS = (2, 4, 8, 16)
POOL_HALO = 8
ATTN_HEADS = 8
ATTN_HEAD_DIM = 64
ATTN_V_DIM = 128
QK_WIDTH = 1024
ATTN_WIDTH = 1024
IN_PROJ_WIDTH = 4096
ROPE_THETA = 10000.0
PEER_HEADS = 8
PEER_KEYS = 128
PEER_EXPERTS = PEER_KEYS * PEER_KEYS
PEER_TOPK = 16
PEER_SLOTS = PEER_HEADS * PEER_TOPK
EPS = 1e-6
LAMBDA_INIT = 0.8 - 0.6 * math.exp(-0.3 * 0)
LOG2E = math.log2(math.e)

LANES = 128
VMEM_LIMIT = 56 * 1024 * 1024


def _cparams(sem):
    return pltpu.CompilerParams(dimension_semantics=sem, vmem_limit_bytes=VMEM_LIMIT)


IN_TM = 512
IN_TN = 512


def _inproj_kernel(x_ref, g_ref, w_ref, cos_ref, sin_ref, z_ref, h_scr):
    j = pl.program_id(1)

    @pl.when(j == 0)
    def _():
        x = x_ref[...]
        ms = jnp.mean(x * x, axis=-1, keepdims=True)
        h_scr[...] = (x * lax.rsqrt(ms + EPS) * g_ref[...]).astype(BF16)

    acc = jnp.dot(h_scr[...], w_ref[...], preferred_element_type=F32)
    q_lo, k_lo, v_lo = (POOL_WIDTH // IN_TN, (POOL_WIDTH + QK_WIDTH) // IN_TN,
                        (POOL_WIDTH + 2 * QK_WIDTH) // IN_TN)
    is_rope = jnp.logical_and(j >= q_lo, j < v_lo)

    @pl.when(is_rope)
    def _():
        cos = cos_ref[...]
        sin = sin_ref[...]
        scale = jnp.where(j < k_lo, ATTN_HEAD_DIM ** -0.5 * LOG2E, 1.0).astype(F32)
        lane = lax.broadcasted_iota(jnp.int32, cos.shape, 1)
        lower = (lane % ATTN_HEAD_DIM) < (ATTN_HEAD_DIM // 2)
        for c in range(IN_TN // LANES):
            xc = acc[:, c * LANES:(c + 1) * LANES]
            up = pltpu.roll(xc, LANES - ATTN_HEAD_DIM // 2, 1)
            dn = pltpu.roll(xc, ATTN_HEAD_DIM // 2, 1)
            rot = jnp.where(lower, -up, dn)
            z_ref[:, c * LANES:(c + 1) * LANES] = ((xc * cos + rot * sin) * scale).astype(BF16)

    @pl.when(jnp.logical_not(is_rope))
    def _():
        z_ref[...] = acc.astype(BF16)


def _in_proj(x, g, w_bf, cos, sin):
    S = x.shape[0]
    return pl.pallas_call(
        _inproj_kernel,
        grid=(S // IN_TM, IN_PROJ_WIDTH // IN_TN),
        in_specs=[
            pl.BlockSpec((IN_TM, D_MODEL), lambda i, j: (i, 0)),
            pl.BlockSpec((1, D_MODEL), lambda i, j: (0, 0)),
            pl.BlockSpec((D_MODEL, IN_TN), lambda i, j: (0, j)),
            pl.BlockSpec((IN_TM, LANES), lambda i, j: (i, 0)),
            pl.BlockSpec((IN_TM, LANES), lambda i, j: (i, 0)),
        ],
        out_specs=pl.BlockSpec((IN_TM, IN_TN), lambda i, j: (i, j)),
        out_shape=jax.ShapeDtypeStruct((S, IN_PROJ_WIDTH), BF16),
        scratch_shapes=[pltpu.VMEM((IN_TM, D_MODEL), BF16)],
        compiler_params=_cparams(("arbitrary", "arbitrary")),
        name="in_proj",
    )(x, g, w_bf, cos, sin)


AT_TQ = 256
AT_TK = 512


def _attn_kernel(q_ref, k_ref, v_ref, lam_ref, g_ref, o_ref,
                 qs_scr, s_scr, p_scr, alpha_scr, m_scr, acc_scr):
    n_chunks = k_ref.shape[0] // AT_TK
    assert n_chunks % 2 == 0, "the pipeline alternates two buffer sets"
    q = q_ref[...]
    lane = lax.broadcasted_iota(jnp.int32, q.shape, 1)
    zero = jnp.zeros_like(q)
    qs_scr[...] = jnp.concatenate([jnp.where(lane < ATTN_HEAD_DIM, q, zero),
                                   jnp.where(lane >= ATTN_HEAD_DIM, q, zero)], axis=0)
    m_scr[...] = jnp.full(m_scr.shape, -jnp.inf, F32)
    acc_scr[...] = jnp.zeros(acc_scr.shape, F32)
    p_scr[1] = jnp.zeros(p_scr.shape[1:], BF16)
    alpha_scr[1] = jnp.ones(alpha_scr.shape[1:], F32)
    ones = jnp.ones((AT_TK, LANES), BF16)

    def scores(c):
        off = pl.multiple_of(c * AT_TK, AT_TK)
        return lax.dot_general(qs_scr[...], k_ref[pl.ds(off, AT_TK), :], (((1,), (1,)), ((), ())),
                               preferred_element_type=F32)

    def accumulate(c, b):
        off = pl.multiple_of(c * AT_TK, AT_TK)
        v_aug = jnp.concatenate([v_ref[pl.ds(off, AT_TK), :], ones], axis=1)
        pv = jnp.dot(p_scr[b], v_aug, preferred_element_type=F32)
        acc_scr[...] = jnp.tile(alpha_scr[b], (1, 2)) * acc_scr[...] + pv

    def stage(c, b):
        s_scr[1 - b] = scores(jnp.minimum(c + 1, n_chunks - 1))
        accumulate(jnp.maximum(c - 1, 0), 1 - b)
        s = s_scr[b]
        m_old = m_scr[...]
        m_new = jnp.maximum(m_old, jnp.max(s, axis=-1, keepdims=True))
        alpha_scr[b] = jnp.exp2(m_old - m_new)
        p_scr[b] = jnp.exp2(s - jnp.tile(m_new, (1, AT_TK // LANES))).astype(BF16)
        m_scr[...] = m_new

    s_scr[0] = scores(0)

    def body(c2, carry):
        stage(2 * c2, 0)
        stage(2 * c2 + 1, 1)
        return carry

    lax.fori_loop(0, n_chunks // 2, body, 0)
    accumulate(n_chunks - 1, 1)

    lv = lam_ref[...]
    lam = (jnp.exp(jnp.sum(lv[0:1] * lv[1:2], axis=-1, keepdims=True))
           - jnp.exp(jnp.sum(lv[2:3] * lv[3:4], axis=-1, keepdims=True)) + LAMBDA_INIT)
    acc = acc_scr[...]
    o = acc[:, :ATTN_V_DIM] / acc[:, ATTN_V_DIM:]
    d = o[:AT_TQ] - lam * o[AT_TQ:]
    ms = jnp.mean(d * d, axis=-1, keepdims=True)
    y = d * lax.rsqrt(ms + EPS) * g_ref[...] * (1.0 - LAMBDA_INIT)
    o_ref[...] = y.astype(BF16)


def _attention(z, lam_vecs, subln_g):
    S = z.shape[0]
    qb, kb, vb = POOL_WIDTH // LANES, (POOL_WIDTH + QK_WIDTH) // LANES, (POOL_WIDTH + 2 * QK_WIDTH) // LANES
    return pl.pallas_call(
        _attn_kernel,
        grid=(ATTN_HEADS, S // AT_TQ),
        in_specs=[
            pl.BlockSpec((AT_TQ, LANES), lambda h, i: (i, qb + h)),
            pl.BlockSpec((S, LANES), lambda h, i: (0, kb + h)),
            pl.BlockSpec((S, LANES), lambda h, i: (0, vb + h)),
            pl.BlockSpec((4, ATTN_HEAD_DIM), lambda h, i: (0, 0)),
            pl.BlockSpec((1, ATTN_V_DIM), lambda h, i: (0, 0)),
        ],
        out_specs=pl.BlockSpec((AT_TQ, LANES), lambda h, i: (i, h)),
        out_shape=jax.ShapeDtypeStruct((S, ATTN_WIDTH), BF16),
        scratch_shapes=[pltpu.VMEM((2 * AT_TQ, LANES), BF16),
                        pltpu.VMEM((2, 2 * AT_TQ, AT_TK), F32),
                        pltpu.VMEM((2, 2 * AT_TQ, AT_TK), BF16),
                        pltpu.VMEM((2, 2 * AT_TQ, LANES), F32),
                        pltpu.VMEM((2 * AT_TQ, LANES), F32),
                        pltpu.VMEM((2 * AT_TQ, 2 * ATTN_V_DIM), F32)],
        compiler_params=_cparams(("arbitrary", "arbitrary")),
        name="attn",
    )(z, z, z, lam_vecs, subln_g)


MX_TM = 256
MX_HB = 16


def _mix_kernel(x_ref, zp_ref, zm_ref, zn_ref, a_ref, pw_ref, pb_ref, ps_ref, wo_ref, g2_ref,
                x1_ref, xn_ref, *, seq_len):
    i = pl.program_id(0)
    r0 = i * MX_TM
    main = zm_ref[...].astype(F32)
    ext = jnp.concatenate([zp_ref[...].astype(F32)[MX_HB - POOL_HALO:], main,
                           zn_ref[...].astype(F32)[:POOL_HALO]], axis=0)
    ext_bf = ext.astype(BF16)
    ne = MX_TM + 2 * POOL_HALO
    t = r0 + lax.broadcasted_iota(jnp.int32, (MX_TM, ne), 0)
    r = r0 - POOL_HALO + lax.broadcasted_iota(jnp.int32, (MX_TM, ne), 1)
    tcol = r0 + lax.broadcasted_iota(jnp.int32, (MX_TM, 1), 0)
    acc = jnp.dot(a_ref[...], wo_ref[POOL_WIDTH:, :], preferred_element_type=F32)
    for gi, w in enumerate(POOL_WINDOWS):
        lo = jnp.maximum(t - w // 2, 0)
        hi = jnp.minimum(t + w // 2, seq_len)
        band = jnp.where(jnp.logical_and(r >= lo, r < hi), 1.0, 0.0).astype(BF16)
        cnt = (jnp.minimum(tcol + w // 2, seq_len) - jnp.maximum(tcol - w // 2, 0)).astype(F32)
        cs = slice(gi * POOL_GROUP_WIDTH, (gi + 1) * POOL_GROUP_WIDTH)
        wsum = jnp.dot(band, ext_bf[:, cs], preferred_element_type=F32)
        pooled = wsum / cnt - main[:, cs]
        y = jnp.dot(pooled.astype(BF16), pw_ref[gi], preferred_element_type=F32)
        y = (y + pb_ref[:, cs]) * ps_ref[:, cs]
        acc = acc + jnp.dot(y.astype(BF16), wo_ref[cs, :], preferred_element_type=F32)
    x1 = x_ref[...] + acc
    x1_ref[...] = x1
    ms = jnp.mean(x1 * x1, axis=-1, keepdims=True)
    xn_ref[...] = x1 * lax.rsqrt(ms + EPS) * g2_ref[...]


def _mix(x, z, a, pool_w_bf, pool_b, pool_scale, w_o_bf, norm2_g):
    S = x.shape[0]
    hb = MX_TM // MX_HB
    nhb = S // MX_HB
    return pl.pallas_call(
        functools.partial(_mix_kernel, seq_len=S),
        grid=(S // MX_TM,),
        in_specs=[
            pl.BlockSpec((MX_TM, D_MODEL), lambda i: (i, 0)),
            pl.BlockSpec((MX_HB, POOL_WIDTH), lambda i: (jnp.maximum(i * hb - 1, 0), 0)),
            pl.BlockSpec((MX_TM, POOL_WIDTH), lambda i: (i, 0)),
            pl.BlockSpec((MX_HB, POOL_WIDTH), lambda i: (jnp.minimum((i + 1) * hb, nhb - 1), 0)),
            pl.BlockSpec((MX_TM, ATTN_WIDTH), lambda i: (i, 0)),
            pl.BlockSpec((POOL_GROUPS, POOL_GROUP_WIDTH, POOL_GROUP_WIDTH), lambda i: (0, 0, 0)),
            pl.BlockSpec((1, POOL_WIDTH), lambda i: (0, 0)),
            pl.BlockSpec((1, POOL_WIDTH), lambda i: (0, 0)),
            pl.BlockSpec((D_MODEL, D_MODEL), lambda i: (0, 0)),
            pl.BlockSpec((1, D_MODEL), lambda i: (0, 0)),
        ],
        out_specs=[pl.BlockSpec((MX_TM, D_MODEL), lambda i: (i, 0)),
                   pl.BlockSpec((MX_TM, D_MODEL), lambda i: (i, 0))],
        out_shape=[jax.ShapeDtypeStruct((S, D_MODEL), F32), jax.ShapeDtypeStruct((S, D_MODEL), F32)],
        compiler_params=_cparams(("arbitrary",)),
        name="mix",
    )(x, z, z, z, a, pool_w_bf, pool_b, pool_scale, w_o_bf, norm2_g)


RT_TM = 256


def _top1(s, pos, sentinel):
    m = jnp.max(s, axis=0, keepdims=True)
    p = jnp.min(jnp.where(s == m, pos, sentinel), axis=0, keepdims=True)
    return m, p


def _route_kernel(xn_ref, wq_ref, keys_ref, idx_ref, gate_ref):
    q = jnp.dot(xn_ref[...].astype(BF16), wq_ref[...], preferred_element_type=F32)
    krow = lax.broadcasted_iota(jnp.int32, (PEER_KEYS, RT_TM), 0)
    nc = PEER_TOPK * PEER_TOPK
    half = PEER_TOPK // 2
    n_kept = PEER_TOPK + (half - 1) * half + half
    r = lax.broadcasted_iota(jnp.int32, (n_kept, RT_TM), 0)
    mid = r - PEER_TOPK
    cpos = jnp.where(r < PEER_TOPK, r,
                     jnp.where(r < n_kept - half, (1 + (mid >> (half.bit_length() - 1))) * PEER_TOPK + (mid & (half - 1)),
                               (half + r - (n_kept - half)) * PEER_TOPK))
    for h in range(PEER_HEADS):
        sv, si = [], []
        for c in range(2):
            hc = 2 * h + c
            qc = q[:, hc * PEER_KEYS:(hc + 1) * PEER_KEYS].astype(BF16)
            s = lax.dot_general(keys_ref[hc], qc, (((1,), (1,)), ((), ())),
                                preferred_element_type=F32)
            vals, ids = [], []
            for _ in range(PEER_TOPK):
                m, p = _top1(s, krow, PEER_KEYS)
                vals.append(m)
                ids.append(p)
                s = jnp.where(krow == p, -jnp.inf, s)
            sv.append(vals)
            si.append(ids)
        s2 = jnp.concatenate(sv[1], axis=0)
        i2 = jnp.concatenate(si[1], axis=0)
        s1_hi = jnp.concatenate(sv[0][half:], axis=0)
        i1_hi = jnp.concatenate(si[0][half:], axis=0)
        cand = jnp.concatenate([sv[0][0] + s2] + [sv[0][a] + s2[:half] for a in range(1, half)]
                               + [s1_hi + sv[1][0]], axis=0)
        cidx = jnp.concatenate([si[0][0] * PEER_KEYS + i2]
                               + [si[0][a] * PEER_KEYS + i2[:half] for a in range(1, half)]
                               + [i1_hi * PEER_KEYS + si[1][0]], axis=0)
        tops, tidx = [], []
        for _ in range(PEER_TOPK):
            m, p = _top1(cand, cpos, nc)
            hit = cpos == p
            tops.append(m)
            tidx.append(jnp.max(jnp.where(hit, cidx, -1), axis=0, keepdims=True))
            cand = jnp.where(hit, -jnp.inf, cand)
        ts = jnp.concatenate(tops, axis=0)
        e = jnp.exp(ts - tops[0])
        gate = e / jnp.sum(e, axis=0, keepdims=True)
        rows = slice(h * PEER_TOPK, (h + 1) * PEER_TOPK)
        idx_ref[rows, :] = jnp.concatenate(tidx, axis=0)
        gate_ref[rows, :] = gate


def _route(xn, wq_bf, keys_bf):
    T = xn.shape[0]
    return pl.pallas_call(
        _route_kernel,
        grid=(T // RT_TM,),
        in_specs=[
            pl.BlockSpec((RT_TM, D_MODEL), lambda i: (i, 0)),
            pl.BlockSpec((D_MODEL, D_MODEL), lambda i: (0, 0)),
            pl.BlockSpec((2 * PEER_HEADS, PEER_KEYS, PEER_KEYS), lambda i: (0, 0, 0)),
        ],
        out_specs=[pl.BlockSpec((PEER_SLOTS, RT_TM), lambda i: (0, i)),
                   pl.BlockSpec((PEER_SLOTS, RT_TM), lambda i: (0, i))],
        out_shape=[jax.ShapeDtypeStruct((PEER_SLOTS, T), jnp.int32),
                   jax.ShapeDtypeStruct((PEER_SLOTS, T), F32)],
        compiler_params=_cparams(("arbitrary",)),
        name="route",
    )(xn, wq_bf, keys_bf)


PE_TB = 256
PE_NSLOT = 8
PE_GROUP = 8
PE_BATCH = PE_NSLOT // 2
assert PE_GROUP == PE_NSLOT and PE_TB % PE_GROUP == 0


def _peer_kernel(idx_ref, gate_ref, xn_ref, x1_ref, fg_ref, uv_hbm, y_ref, buf, sem):
    def issue_batch(t_first, slot_first):
        for b in range(PE_BATCH):
            slot = (slot_first + b) % PE_NSLOT
            for k in range(PEER_SLOTS):
                pltpu.make_async_copy(uv_hbm.at[idx_ref[t_first + b, k]], buf.at[slot, k],
                                      sem.at[slot]).start()

    def wait(slot):
        pltpu.make_async_copy(uv_hbm.at[pl.ds(0, PEER_SLOTS)], buf.at[slot], sem.at[slot]).wait()

    issue_batch(0, 0)

    gates = gate_ref[...]
    tok_lane = lax.broadcasted_iota(jnp.int32, gates.shape, 1)
    fg = fg_ref[...]

    def group(t8, carry):
        base = pl.multiple_of(t8 * PE_GROUP, PE_GROUP)
        xn8 = xn_ref[pl.ds(base, PE_GROUP), :]
        x18 = x1_ref[pl.ds(base, PE_GROUP), :]
        rows = []
        for j in range(PE_GROUP):
            t = base + j
            if j % PE_BATCH == 0:
                @pl.when(t + PE_BATCH < PE_TB)
                def _():
                    issue_batch(t + PE_BATCH, j + PE_BATCH)
            slot = j % PE_NSLOT
            wait(slot)
            u = buf[slot, :, :D_MODEL]
            hcol = jnp.sum(u * xn8[j:j + 1, :], axis=-1, keepdims=True)
            gcol = jnp.sum(jnp.where(tok_lane == t, gates, 0.0), axis=-1, keepdims=True)
            wcol = gcol * (0.5 * hcol * (1.0 + lax.erf(hcol * (2.0 ** -0.5))))
            v = buf[slot, :, D_MODEL:]
            rows.append(jnp.sum(wcol * v, axis=0, keepdims=True))
        x2 = x18 + jnp.concatenate(rows, axis=0)
        ms = jnp.mean(x2 * x2, axis=-1, keepdims=True)
        y_ref[pl.ds(base, PE_GROUP), :] = x2 * lax.rsqrt(ms + EPS) * fg
        return carry

    lax.fori_loop(0, PE_TB // PE_GROUP, group, 0)


def _peer(idx, gate_t, xn, x1, fg, uv):
    T = xn.shape[0]
    return pl.pallas_call(
        _peer_kernel,
        grid=(T // PE_TB,),
        in_specs=[
            pl.BlockSpec((PE_TB, PEER_SLOTS), lambda i: (i, 0), memory_space=pltpu.SMEM),
            pl.BlockSpec((PEER_SLOTS, PE_TB), lambda i: (0, i)),
            pl.BlockSpec((PE_TB, D_MODEL), lambda i: (i, 0)),
            pl.BlockSpec((PE_TB, D_MODEL), lambda i: (i, 0)),
            pl.BlockSpec((1, D_MODEL), lambda i: (0, 0)),
            pl.BlockSpec(memory_space=pl.ANY),
        ],
        out_specs=pl.BlockSpec((PE_TB, D_MODEL), lambda i: (i, 0)),
        out_shape=jax.ShapeDtypeStruct((T, D_MODEL), F32),
        scratch_shapes=[pltpu.VMEM((PE_NSLOT, PEER_SLOTS, 2 * D_MODEL), F32),
                        pltpu.SemaphoreType.DMA((PE_NSLOT,))],
        compiler_params=_cparams(("arbitrary",)),
        name="peer",
    )(idx, gate_t, xn, x1, fg, uv)


def _rope_tables(S):
    half = ATTN_HEAD_DIM // 2
    inv = 1.0 / (ROPE_THETA ** (jnp.arange(half, dtype=F32) / half))
    ang = jnp.arange(S, dtype=F32)[:, None] * inv[None, :]
    reps = LANES // half
    return jnp.tile(jnp.cos(ang), (1, reps)), jnp.tile(jnp.sin(ang), (1, reps))


def _trunk(x, p):
    S = x.shape[0]
    cos, sin = _rope_tables(S)
    z = _in_proj(x, p["norm1_g"], p["w_in"], cos, sin)
    a = _attention(z, p["lam_vecs"], p["subln_g"])
    x1, xn = _mix(x, z, a, p["pool_w"], p["pool_b"], p["pool_scale"], p["w_o"], p["norm2_g"])
    idx_t, gate_t = _route(xn, p["peer_wq"], p["peer_keys"])
    return _peer(idx_t.T, gate_t, xn, x1, p["final_norm_g"], p["uv"])


def kernel(x_prompt, x_sample, norm1_g, w_in, pool_w, pool_b, pool_scale, lambda_q1, lambda_k1,
           lambda_q2, lambda_k2, subln_g, w_o, norm2_g, peer_wq, peer_keys, peer_u, peer_v, final_norm_g):
    l = 0
    p = {
        "norm1_g": norm1_g[l][None, :],
        "w_in": w_in[l].astype(BF16),
        "pool_w": pool_w[l].astype(BF16),
        "pool_b": pool_b[l].reshape(1, POOL_WIDTH),
        "pool_scale": pool_scale[l][None, :],
        "lam_vecs": jnp.stack([lambda_q1[l], lambda_k1[l], lambda_q2[l], lambda_k2[l]]),
        "subln_g": subln_g[l][None, :],
        "w_o": w_o[l].astype(BF16),
        "norm2_g": norm2_g[l][None, :],
        "peer_wq": peer_wq[l].astype(BF16),
        "peer_keys": peer_keys[l].reshape(2 * PEER_HEADS, PEER_KEYS, PEER_KEYS).astype(BF16),
        "uv": jnp.concatenate([peer_u[l], peer_v[l]], axis=1),
        "final_norm_g": final_norm_g[None, :],
    }
    y_prompt = _trunk(x_prompt[0], p)[None]
    y_sample = _trunk(x_sample[0], p)[None]
    return (y_prompt, y_sample)
```
